```python
import math
import jax, jax.numpy as jnp
from jax import lax
import numpy as np

D_MODEL = 4096
BATCH = 8
SEQ = 2048
DEPTH = 2

CHUNK = 64
N_GROUPS = 4
GROUP_W = D_MODEL // N_GROUPS
D_MIX = N_GROUPS * GROUP_W
SGU_CHUNK = 128
SGU_HEADS = 8
CONV_WIDTH = 31
CONV_GROUPS = 8
RET_HEADS = 4
RET_HEAD_DIM = GROUP_W // RET_HEADS
MEM_LEN = 256
MEM_HEADS = 4
MEM_HEAD_DIM = GROUP_W // MEM_HEADS
ROPE_BASE = 10000.0
EPS = 1e-6
N_IN_SLICES = 12
W_IN_COLS = N_IN_SLICES * GROUP_W

kernel_name = "hybrid_sgu_conformer_retention_memory_block"


def rms_norm(x, g):
    xf = x.astype(jnp.float32)
    y = xf * lax.rsqrt(jnp.mean(xf * xf, axis=-1, keepdims=True) + EPS)
    return (y * g.astype(jnp.float32)).astype(x.dtype)


def group_layer_norm(x, n_groups, g, b):
    shp = x.shape
    xf = x.astype(jnp.float32).reshape(shp[:-1] + (n_groups, shp[-1] // n_groups))
    mu = jnp.mean(xf, axis=-1, keepdims=True)
    var = jnp.mean(jnp.square(xf - mu), axis=-1, keepdims=True)
    y = ((xf - mu) * lax.rsqrt(var + EPS)).reshape(shp)
    return (y * g.astype(jnp.float32) + b.astype(jnp.float32)).astype(x.dtype)


def sgu_spatial_mix(v, w_s, b_s):
    bsz, s_len, _ = v.shape
    n = s_len // SGU_CHUNK
    vb = v.reshape(bsz, n, SGU_CHUNK, SGU_HEADS, GROUP_W // SGU_HEADS)
    mask = jnp.tril(jnp.ones((SGU_CHUNK, SGU_CHUNK), v.dtype))
    out = jnp.einsum('gts,bnsgc->bntgc', w_s * mask, vb) + b_s.T[None, None, :, :, None]
    return out.reshape(bsz, s_len, GROUP_W)


def causal_depthwise_conv(x, w, b):
    y = lax.conv_general_dilated(
        x, w[:, None, :], window_strides=(1,), padding=[(CONV_WIDTH - 1, 0)],
        dimension_numbers=('NWC', 'WIO', 'NWC'), feature_group_count=x.shape[-1])
    return y + b


def rotary(x, cos, sin):
    x1, x2 = jnp.split(x, 2, axis=-1)
    return jnp.concatenate([x1 * cos - x2 * sin, x2 * cos + x1 * sin], axis=-1)


def retention_chunkwise(q, k, v):
    bsz, s_len, n_h, d = q.shape
    n = s_len // CHUNK
    dt = v.dtype
    log_g = jnp.log(1.0 - jnp.power(2.0, -5.0 - jnp.arange(n_h, dtype=jnp.float32)))
    t = jnp.arange(CHUNK, dtype=jnp.float32)
    diff = t[:, None] - t[None, :]
    intra_decay = jnp.where(diff >= 0, jnp.exp(log_g[:, None, None] * jnp.maximum(diff, 0.0)), 0.0).astype(dt)
    k_decay = jnp.exp(log_g[:, None] * (CHUNK - 1 - t)[None, :]).astype(dt)
    q_decay = jnp.exp(log_g[:, None] * (t + 1)[None, :]).astype(dt)
    chunk_decay = jnp.exp(log_g * CHUNK).astype(dt)
    qc = q.reshape(bsz, n, CHUNK, n_h, d)
    kc = k.reshape(bsz, n, CHUNK, n_h, d)
    vc = v.reshape(bsz, n, CHUNK, n_h, d)
    scores = jnp.einsum('bnthd,bnshd->bnhts', qc, kc) * intra_decay
    intra = jnp.einsum('bnhts,bnshe->bnthe', scores, vc)
    chunk_kv = jnp.einsum('bnshd,hs,bnshe->bnhde', kc, k_decay, vc)

    def step(state, kv):
        return chunk_decay[None, :, None, None] * state + kv, state

    _, prev = lax.scan(step, jnp.zeros((bsz, n_h, d, d), dt), jnp.moveaxis(chunk_kv, 1, 0))
    prev = jnp.moveaxis(prev, 0, 1)
    cross = jnp.einsum('bnthd,ht,bnhde->bnthe', qc, q_decay, prev)
    return (intra + cross).reshape(bsz, s_len, n_h, d)


def memory_attention(q, mem_n, w_mem_kv):
    bsz, m_len, _ = mem_n.shape
    kv = jnp.einsum('bmd,de->bme', mem_n, w_mem_kv)
    k_m, v_m = jnp.split(kv, 2, axis=-1)
    k_m = k_m.reshape(bsz, m_len, MEM_HEADS, MEM_HEAD_DIM)
    v_m = v_m.reshape(bsz, m_len, MEM_HEADS, MEM_HEAD_DIM)
    s = jnp.einsum('bshd,bmhd->bhsm', q, k_m).astype(jnp.float32) * (MEM_HEAD_DIM ** -0.5)
    p = jax.nn.softmax(s, axis=-1).astype(v_m.dtype)
    return jnp.einsum('bhsm,bmhd->bshd', p, v_m).reshape(q.shape[0], q.shape[1], GROUP_W)


def setup_inputs(seed: int = 0) -> dict:
    key = jax.random.key(seed)
    ks = jax.random.split(key, 24)
    f32 = jnp.float32
    nrm = lambda k, shp, sc: jax.random.normal(k, shp, f32) * sc
    x = nrm(ks[0], (BATCH, SEQ, D_MODEL), 1.0)
    mem = nrm(ks[1], (BATCH, MEM_LEN, D_MODEL), 1.0)
    start = jax.random.randint(ks[2], (BATCH, 1), 0, 4096, dtype=jnp.int32)
    positions = (start + jnp.arange(SEQ, dtype=jnp.int32)[None, :]).astype(jnp.int32)
    return {
        "x": x,
        "mem": mem,
        "positions": positions,
        "norm_g": 1.0 + nrm(ks[3], (DEPTH, D_MODEL), 0.02),
        "w_in": nrm(ks[4], (DEPTH, D_MODEL, W_IN_COLS), D_MODEL ** -0.5),
        "sgu_norm_g": 1.0 + nrm(ks[5], (DEPTH, GROUP_W), 0.02),
        "sgu_norm_b": nrm(ks[6], (DEPTH, GROUP_W), 0.02),
        "sgu_w": nrm(ks[7], (DEPTH, SGU_HEADS, SGU_CHUNK, SGU_CHUNK), 0.5 * SGU_CHUNK ** -0.5),
        "sgu_b": 1.0 + nrm(ks[8], (DEPTH, SGU_HEADS, SGU_CHUNK), 0.02),
        "conv_w": nrm(ks[9], (DEPTH, CONV_WIDTH, GROUP_W), CONV_WIDTH ** -0.5),
        "conv_b": nrm(ks[10], (DEPTH, GROUP_W), 0.02),
        "conv_norm_g": 1.0 + nrm(ks[11], (DEPTH, GROUP_W), 0.02),
        "conv_norm_b": nrm(ks[12], (DEPTH, GROUP_W), 0.02),
        "ret_norm_g": 1.0 + nrm(ks[13], (DEPTH, GROUP_W), 0.02),
        "ret_norm_b": nrm(ks[14], (DEPTH, GROUP_W), 0.02),
        "mem_norm_g": 1.0 + nrm(ks[15], (DEPTH, D_MODEL), 0.02),
        "w_mem_kv": nrm(ks[16], (DEPTH, D_MODEL, 2 * GROUP_W), D_MODEL ** -0.5),
        "w_out": nrm(ks[17], (DEPTH, D_MIX, D_MODEL), D_MIX ** -0.5),
        "final_norm_g": 1.0 + nrm(ks[18], (D_MODEL,), 0.02),
    }


def reference(x, mem, positions, norm_g, w_in, sgu_norm_g, sgu_norm_b, sgu_w, sgu_b,
              conv_w, conv_b, conv_norm_g, conv_norm_b, ret_norm_g, ret_norm_b,
              mem_norm_g, w_mem_kv, w_out, final_norm_g):
    bsz, s_len, _ = x.shape
    half = RET_HEAD_DIM // 2
    inv_freq = jnp.power(ROPE_BASE, -jnp.arange(half, dtype=jnp.float32) / half)
    ang = positions.astype(jnp.float32)[..., None] * inv_freq
    cos = jnp.cos(ang)[:, :, None, :].astype(x.dtype)
    sin = jnp.sin(ang)[:, :, None, :].astype(x.dtype)

    for l in range(DEPTH):
        h = rms_norm(x, norm_g[l])
        proj = jnp.einsum('bsd,de->bse', h, w_in[l])
        (a_u, a_v, a_g, b_a, b_b, b_g,
         c_q, c_k, c_v, c_g, m_q, m_g) = jnp.split(proj, N_IN_SLICES, axis=-1)

        a_v = group_layer_norm(a_v, 1, sgu_norm_g[l], sgu_norm_b[l])
        y_a = a_u * sgu_spatial_mix(a_v, sgu_w[l], sgu_b[l]) * jax.nn.silu(a_g)

        glu = b_a * jax.nn.sigmoid(b_b)
        cv = causal_depthwise_conv(glu, conv_w[l], conv_b[l])
        cv = group_layer_norm(cv, CONV_GROUPS, conv_norm_g[l], conv_norm_b[l])
        y_b = jax.nn.silu(cv) * jax.nn.silu(b_g)

        q = rotary(c_q.reshape(bsz, s_len, RET_HEADS, RET_HEAD_DIM), cos, sin)
        k = rotary(c_k.reshape(bsz, s_len, RET_HEADS, RET_HEAD_DIM), cos, sin) * (RET_HEAD_DIM ** -0.5)
        v = c_v.reshape(bsz, s_len, RET_HEADS, RET_HEAD_DIM)
        r = retention_chunkwise(q, k, v).reshape(bsz, s_len, GROUP_W)
        r = group_layer_norm(r, RET_HEADS, ret_norm_g[l], ret_norm_b[l])
        y_c = r * jax.nn.silu(c_g)

        mem_n = rms_norm(mem, mem_norm_g[l])
        mq = m_q.reshape(bsz, s_len, MEM_HEADS, MEM_HEAD_DIM)
        y_m = memory_attention(mq, mem_n, w_mem_kv[l]) * jax.nn.silu(m_g)

        y = jnp.concatenate([y_a, y_b, y_c, y_m], axis=-1)
        x = x + jnp.einsum('bse,ed->bsd', y, w_out[l])

    return rms_norm(x, final_norm_g)
```

```python
import functools

import jax
import jax.numpy as jnp
from jax import lax
from jax.experimental import pallas as pl
from jax.experimental.pallas import tpu as pltpu

F32 = jnp.float32
BF16 = jnp.bfloat16

D_MODEL = 4096
GROUP_W = 1024
N_IN_SLICES = 12
SGU_CHUNK = 128
SGU_HEADS = 8
CONV_WIDTH = 31
CONV_GROUPS = 8
RET_HEADS = 4
RET_HEAD_DIM = 256
MEM_LEN = 256
MEM_HEADS = 4
MEM_HEAD_DIM = 256
ROPE_BASE = 10000.0
EPS = 1e-6

V7X_VMEM_BYTES = 64 * 1024 * 1024
LANES = 128
SUBLANES = 8

NORM_ROWS = 256
MM_BM = 1024
MM_BN = 1024
MIX_ROWS = 256
RET_CHUNK = 256
CONV_HALO = 32
CONV_RC = 64
MM_VMEM_LIMIT = 56 * 1024 * 1024
MIX_VMEM_LIMIT = 48 * 1024 * 1024


def _sigmoid(x):
    return 1.0 / (1.0 + jnp.exp(-x))


def _silu(x):
    return x * _sigmoid(x)


def _params(semantics, vmem_limit):
    return pltpu.CompilerParams(dimension_semantics=semantics, vmem_limit_bytes=vmem_limit)


def _rmsnorm_kernel(x_ref, g_ref, o_ref):
    x = x_ref[...]
    ms = jnp.mean(x * x, axis=-1, keepdims=True)
    o_ref[...] = ((x * lax.rsqrt(ms + EPS)) * g_ref[...]).astype(o_ref.dtype)


def _rmsnorm(x2d, g, out_dtype):
    rows, d = x2d.shape
    return pl.pallas_call(
        _rmsnorm_kernel,
        grid=(rows // NORM_ROWS,),
        in_specs=[pl.BlockSpec((NORM_ROWS, d), lambda i: (i, 0)),
                  pl.BlockSpec((1, d), lambda i: (0, 0))],
        out_specs=pl.BlockSpec((NORM_ROWS, d), lambda i: (i, 0)),
        out_shape=jax.ShapeDtypeStruct((rows, d), out_dtype),
        compiler_params=_params(("parallel",), MIX_VMEM_LIMIT),
        name="rmsnorm",
    )(x2d, g.reshape(1, d))


def _matmul_kernel(a_ref, b_ref, o_ref):
    o_ref[...] = jnp.dot(a_ref[...], b_ref[...], preferred_element_type=F32).astype(o_ref.dtype)


def _matmul(a, b, out_dtype, bm, bn):
    m, k = a.shape
    _, n = b.shape
    bm = min(bm, m)
    return pl.pallas_call(
        _matmul_kernel,
        grid=(m // bm, n // bn),
        in_specs=[pl.BlockSpec((bm, k), lambda i, j: (i, 0)),
                  pl.BlockSpec((k, bn), lambda i, j: (0, j))],
        out_specs=pl.BlockSpec((bm, bn), lambda i, j: (i, j)),
        out_shape=jax.ShapeDtypeStruct((m, n), out_dtype),
        compiler_params=_params(("parallel", "arbitrary"), MM_VMEM_LIMIT),
        name="matmul",
    )(a, b)


def _outproj_kernel(ya_ref, yb_ref, yc_ref, ym_ref, w_ref, x_ref, o_ref):
    acc = x_ref[...]
    for g, y_ref in enumerate((ya_ref, yb_ref, yc_ref, ym_ref)):
        acc = acc + jnp.dot(y_ref[...], w_ref[g * GROUP_W:(g + 1) * GROUP_W, :],
                            preferred_element_type=F32)
    o_ref[...] = acc


def _outproj(ys, w, x2d, bm, bn):
    m, d = x2d.shape
    k = w.shape[0]
    y_spec = pl.BlockSpec((bm, GROUP_W), lambda i, j: (i, 0))
    return pl.pallas_call(
        _outproj_kernel,
        grid=(m // bm, d // bn),
        in_specs=[y_spec, y_spec, y_spec, y_spec,
                  pl.BlockSpec((k, bn), lambda i, j: (0, j)),
                  pl.BlockSpec((bm, bn), lambda i, j: (i, j))],
        out_specs=pl.BlockSpec((bm, bn), lambda i, j: (i, j)),
        out_shape=jax.ShapeDtypeStruct((m, d), F32),
        compiler_params=_params(("parallel", "arbitrary"), MM_VMEM_LIMIT),
        name="outproj",
    )(*ys, w, x2d)


def _rope_kernel(pos_ref, freq_ref, cos_ref, sin_ref):
    ang = pos_ref[...] * freq_ref[...]
    cos_ref[...] = jnp.cos(ang)
    sin_ref[...] = jnp.sin(ang)


def _rope_tables(pos_col, inv_freq):
    rows = pos_col.shape[0]
    half = inv_freq.shape[1]
    spec = pl.BlockSpec((MIX_ROWS, half), lambda i: (i, 0))
    return pl.pallas_call(
        _rope_kernel,
        grid=(rows // MIX_ROWS,),
        in_specs=[pl.BlockSpec((MIX_ROWS, 1), lambda i: (i, 0)),
                  pl.BlockSpec((1, half), lambda i: (0, 0))],
        out_specs=[spec, spec],
        out_shape=[jax.ShapeDtypeStruct((rows, half), F32)] * 2,
        compiler_params=_params(("parallel",), MIX_VMEM_LIMIT),
        name="rope_tables",
    )(pos_col, inv_freq)


def _slice_spec(rows, col_block):
    return pl.BlockSpec((rows, GROUP_W), lambda i: (i, col_block))


def _row_spec(n):
    return pl.BlockSpec((1, n), lambda i: (0, 0))


def _sgu_kernel(u_ref, v_ref, g_ref, ng_ref, nb_ref, w_ref, bias_ref, o_ref, vn_ref):
    v = v_ref[...].astype(F32)
    mu = jnp.mean(v, axis=-1, keepdims=True)
    var = jnp.mean(jnp.square(v - mu), axis=-1, keepdims=True)
    vn = ((v - mu) * lax.rsqrt(var + EPS)) * ng_ref[...] + nb_ref[...]
    vn_ref[...] = vn.astype(BF16)

    t = lax.broadcasted_iota(jnp.int32, (SGU_CHUNK, SGU_CHUNK), 0)
    s = lax.broadcasted_iota(jnp.int32, (SGU_CHUNK, SGU_CHUNK), 1)
    causal = t >= s
    for h in range(SGU_HEADS):
        w = jnp.where(causal, w_ref[h], 0.0).astype(BF16)
        cols = slice(h * SGU_CHUNK, (h + 1) * SGU_CHUNK)
        for n in range(MIX_ROWS // SGU_CHUNK):
            rows = slice(n * SGU_CHUNK, (n + 1) * SGU_CHUNK)
            mix = jnp.dot(w, vn_ref[rows, cols], preferred_element_type=F32) + bias_ref[:, cols]
            u = u_ref[rows, cols].astype(F32)
            gate = g_ref[rows, cols].astype(F32)
            o_ref[rows, cols] = ((u * mix) * _silu(gate)).astype(o_ref.dtype)


def _sgu(proj, ng, nb, w, bias_full):
    rows = proj.shape[0]
    return pl.pallas_call(
        _sgu_kernel,
        grid=(rows // MIX_ROWS,),
        in_specs=[_slice_spec(MIX_ROWS, 0), _slice_spec(MIX_ROWS, 1), _slice_spec(MIX_ROWS, 2),
                  _row_spec(GROUP_W), _row_spec(GROUP_W),
                  pl.BlockSpec((SGU_HEADS, SGU_CHUNK, SGU_CHUNK), lambda i: (0, 0, 0)),
                  pl.BlockSpec((SGU_CHUNK, GROUP_W), lambda i: (0, 0))],
        out_specs=pl.BlockSpec((MIX_ROWS, GROUP_W), lambda i: (i, 0)),
        out_shape=jax.ShapeDtypeStruct((rows, GROUP_W), BF16),
        scratch_shapes=[pltpu.VMEM((MIX_ROWS, GROUP_W), BF16)],
        compiler_params=_params(("parallel",), MIX_VMEM_LIMIT),
        name="sgu",
    )(proj, proj, proj, ng.reshape(1, -1), nb.reshape(1, -1), w, bias_full)


def _conv_kernel(steps_per_seq, a_ref, b_ref, g_ref, w_ref, cb_ref, ng_ref, nb_ref, o_ref, buf_ref):
    i = pl.program_id(0)
    first = (i % steps_per_seq) == 0
    n_slabs = GROUP_W // LANES

    def rows2(start, n):
        return pl.ds(2 * start, n, stride=2)

    @pl.when(first)
    def _():
        for s in range(n_slabs):
            buf_ref[s, rows2(0, CONV_HALO), :] = jnp.zeros((CONV_HALO, LANES), F32)

    @pl.when(jnp.logical_not(first))
    def _():
        for s in range(n_slabs):
            buf_ref[s, rows2(0, CONV_HALO), :] = buf_ref[s, rows2(MIX_ROWS, CONV_HALO), :]

    for s in range(n_slabs):
        cols = slice(s * LANES, (s + 1) * LANES)
        buf_ref[s, rows2(CONV_HALO, MIX_ROWS), :] = (
            a_ref[:, cols].astype(F32) * _sigmoid(b_ref[:, cols].astype(F32)))

    def chunk(c, carry):
        r0 = pl.multiple_of(c * CONV_RC, CONV_RC)
        for s in range(n_slabs):
            cols = slice(s * LANES, (s + 1) * LANES)
            acc = jnp.broadcast_to(cb_ref[:, cols], (CONV_RC, LANES))
            for k in range(CONV_WIDTH):
                start = r0 + (CONV_HALO - (CONV_WIDTH - 1) + k)
                acc = acc + w_ref[k:k + 1, cols] * buf_ref[s, rows2(start, CONV_RC), :]
            mu = jnp.mean(acc, axis=-1, keepdims=True)
            var = jnp.mean(jnp.square(acc - mu), axis=-1, keepdims=True)
            y = ((acc - mu) * lax.rsqrt(var + EPS)) * ng_ref[:, cols] + nb_ref[:, cols]
            gate = g_ref[pl.ds(r0, CONV_RC), cols].astype(F32)
            o_ref[pl.ds(r0, CONV_RC), cols] = (_silu(y) * _silu(gate)).astype(o_ref.dtype)
        return carry

    lax.fori_loop(0, MIX_ROWS // CONV_RC, chunk, 0)


def _conv(proj, seq_len, w, cb, ng, nb):
    rows = proj.shape[0]
    kernel = functools.partial(_conv_kernel, seq_len // MIX_ROWS)
    return pl.pallas_call(
        kernel,
        grid=(rows // MIX_ROWS,),
        in_specs=[_slice_spec(MIX_ROWS, 3), _slice_spec(MIX_ROWS, 4), _slice_spec(MIX_ROWS, 5),
                  pl.BlockSpec((CONV_WIDTH, GROUP_W), lambda i: (0, 0)),
                  _row_spec(GROUP_W), _row_spec(GROUP_W), _row_spec(GROUP_W)],
        out_specs=pl.BlockSpec((MIX_ROWS, GROUP_W), lambda i: (i, 0)),
        out_shape=jax.ShapeDtypeStruct((rows, GROUP_W), BF16),
        scratch_shapes=[pltpu.VMEM((GROUP_W // LANES, 2 * (CONV_HALO + MIX_ROWS), LANES), F32)],
        compiler_params=_params(("arbitrary",), MIX_VMEM_LIMIT),
        name="conv",
    )(proj, proj, proj, w, cb.reshape(1, -1), ng.reshape(1, -1), nb.reshape(1, -1))


def _retention_kernel(steps_per_seq, logg_ref, q_ref, k_ref, v_ref, g_ref, cos_ref, sin_ref,
                      ng_ref, nb_ref, o_ref, state_ref):
    i = pl.program_id(0)

    @pl.when((i % steps_per_seq) == 0)
    def _():
        state_ref[...] = jnp.zeros_like(state_ref)

    c = RET_CHUNK
    half = RET_HEAD_DIM // 2
    cos = cos_ref[...]
    sin = sin_ref[...]
    t_row = lax.broadcasted_iota(jnp.int32, (c, c), 0)
    s_col = lax.broadcasted_iota(jnp.int32, (c, c), 1)
    diff = (t_row - s_col).astype(F32)
    t_vec = lax.broadcasted_iota(jnp.int32, (c, 1), 0).astype(F32)

    def rot(ref, lo):
        x1 = ref[:, lo:lo + half].astype(F32)
        x2 = ref[:, lo + half:lo + 2 * half].astype(F32)
        return x1 * cos - x2 * sin, x2 * cos + x1 * sin

    for h in range(RET_HEADS):
        lo = h * RET_HEAD_DIM
        log_g = logg_ref[h]
        q1, q2 = rot(q_ref, lo)
        k1, k2 = rot(k_ref, lo)
        scale = RET_HEAD_DIM ** -0.5
        q = jnp.concatenate([q1, q2], axis=-1)
        k = jnp.concatenate([k1, k2], axis=-1) * scale
        v = v_ref[:, lo:lo + RET_HEAD_DIM].astype(BF16)

        intra_decay = jnp.where(diff >= 0, jnp.exp(log_g * jnp.maximum(diff, 0.0)), 0.0)
        k_decay = jnp.exp(log_g * ((c - 1.0) - t_vec))
        q_decay = jnp.exp(log_g * (t_vec + 1.0))
        chunk_decay = jnp.exp(log_g * float(c))

        scores = lax.dot_general(q.astype(BF16), k.astype(BF16), (((1,), (1,)), ((), ())),
                                 preferred_element_type=F32) * intra_decay
        intra = jnp.dot(scores.astype(BF16), v, preferred_element_type=F32)
        state = state_ref[h]
        cross = jnp.dot((q * q_decay).astype(BF16), state.astype(BF16), preferred_element_type=F32)
        kv = lax.dot_general((k * k_decay).astype(BF16), v, (((0,), (0,)), ((), ())),
                             preferred_element_type=F32)
        state_ref[h] = chunk_decay * state + kv

        r = intra + cross
        mu = jnp.mean(r, axis=-1, keepdims=True)
        var = jnp.mean(jnp.square(r - mu), axis=-1, keepdims=True)
        cols = slice(lo, lo + RET_HEAD_DIM)
        y = ((r - mu) * lax.rsqrt(var + EPS)) * ng_ref[:, cols] + nb_ref[:, cols]
        gate = g_ref[:, cols].astype(F32)
        o_ref[:, cols] = (y * _silu(gate)).astype(o_ref.dtype)


def _retention(proj, seq_len, log_g, cos, sin, ng, nb):
    rows = proj.shape[0]
    half = RET_HEAD_DIM // 2
    kernel = functools.partial(_retention_kernel, seq_len // RET_CHUNK)
    tab_spec = pl.BlockSpec((RET_CHUNK, half), lambda i: (i, 0))
    return pl.pallas_call(
        kernel,
        grid=(rows // RET_CHUNK,),
        in_specs=[pl.BlockSpec(memory_space=pltpu.SMEM),
                  _slice_spec(RET_CHUNK, 6), _slice_spec(RET_CHUNK, 7), _slice_spec(RET_CHUNK, 8),
                  _slice_spec(RET_CHUNK, 9), tab_spec, tab_spec,
                  _row_spec(GROUP_W), _row_spec(GROUP_W)],
        out_specs=pl.BlockSpec((RET_CHUNK, GROUP_W), lambda i: (i, 0)),
        out_shape=jax.ShapeDtypeStruct((rows, GROUP_W), BF16),
        scratch_shapes=[pltpu.VMEM((RET_HEADS, RET_HEAD_DIM, RET_HEAD_DIM), F32)],
        compiler_params=_params(("arbitrary",), MIX_VMEM_LIMIT),
        name="retention",
    )(log_g, proj, proj, proj, proj, cos, sin, ng.reshape(1, -1), nb.reshape(1, -1))


def _memattn_kernel(q_ref, g_ref, k_ref, v_ref, o_ref):
    scale = MEM_HEAD_DIM ** -0.5
    for h in range(MEM_HEADS):
        cols = slice(h * MEM_HEAD_DIM, (h + 1) * MEM_HEAD_DIM)
        q = q_ref[:, cols].astype(BF16)
        s = lax.dot_general(q, k_ref[:, cols].astype(BF16), (((1,), (1,)), ((), ())),
                            preferred_element_type=F32) * scale
        m = jnp.max(s, axis=-1, keepdims=True)
        p = jnp.exp(s - m)
        denom = jnp.sum(p, axis=-1, keepdims=True)
        p = p / denom
        o = jnp.dot(p.astype(BF16), v_ref[:, cols].astype(BF16), preferred_element_type=F32)
        gate = g_ref[:, cols].astype(F32)
        o_ref[:, cols] = (o * _silu(gate)).astype(o_ref.dtype)


def _memattn(proj, kv, seq_len):
    rows = proj.shape[0]
    steps_per_seq = seq_len // MIX_ROWS
    return pl.pallas_call(
        _memattn_kernel,
        grid=(rows // MIX_ROWS,),
        in_specs=[_slice_spec(MIX_ROWS, 10), _slice_spec(MIX_ROWS, 11),
                  pl.BlockSpec((MEM_LEN, GROUP_W), lambda i: (i // steps_per_seq, 0)),
                  pl.BlockSpec((MEM_LEN, GROUP_W), lambda i: (i // steps_per_seq, 1))],
        out_specs=pl.BlockSpec((MIX_ROWS, GROUP_W), lambda i: (i, 0)),
        out_shape=jax.ShapeDtypeStruct((rows, GROUP_W), BF16),
        compiler_params=_params(("parallel",), MIX_VMEM_LIMIT),
        name="memattn",
    )(proj, proj, kv, kv)


def kernel(x, mem, positions, norm_g, w_in, sgu_norm_g, sgu_norm_b, sgu_w, sgu_b,
           conv_w, conv_b, conv_norm_g, conv_norm_b, ret_norm_g, ret_norm_b,
           mem_norm_g, w_mem_kv, w_out, final_norm_g):
    bsz, s_len, d = x.shape
    depth = w_in.shape[0]
    assert d == D_MODEL and w_in.shape[2] == N_IN_SLICES * GROUP_W
    assert s_len % MIX_ROWS == 0 and s_len % RET_CHUNK == 0 and MIX_ROWS % SGU_CHUNK == 0
    assert GROUP_W // CONV_GROUPS == LANES and CONV_HALO >= CONV_WIDTH - 1
    rows = bsz * s_len
    x2d = x.reshape(rows, d)
    mem2d = mem.reshape(bsz * MEM_LEN, d)

    half = RET_HEAD_DIM // 2
    inv_freq = jnp.power(ROPE_BASE, -jnp.arange(half, dtype=F32) / half).reshape(1, half)
    pos_col = positions.astype(F32).reshape(rows, 1)
    cos, sin = _rope_tables(pos_col, inv_freq)
    log_g = jnp.log(1.0 - jnp.power(2.0, -5.0 - jnp.arange(RET_HEADS, dtype=F32)))

    for l in range(depth):
        h = _rmsnorm(x2d, norm_g[l], BF16)
        proj = _matmul(h, w_in[l].astype(BF16), F32, MM_BM, MM_BN)

        bias_full = jnp.repeat(sgu_b[l].T, GROUP_W // SGU_HEADS, axis=1)
        y_a = _sgu(proj, sgu_norm_g[l], sgu_norm_b[l], sgu_w[l], bias_full)
        y_b = _conv(proj, s_len, conv_w[l], conv_b[l], conv_norm_g[l], conv_norm_b[l])
        y_c = _retention(proj, s_len, log_g, cos, sin, ret_norm_g[l], ret_norm_b[l])

        mem_n = _rmsnorm(mem2d, mem_norm_g[l], BF16)
        kv = _matmul(mem_n, w_mem_kv[l].astype(BF16), F32, MM_BM, MM_BN)
        y_m = _memattn(proj, kv, s_len)

        x2d = _outproj((y_a, y_b, y_c, y_m), w_out[l].astype(BF16), x2d, MM_BM, MM_BN)

    out = _rmsnorm(x2d, final_norm_g, F32)
    return out.reshape(bsz, s_len, d)
```

```python
import functools

import jax
import jax.numpy as jnp
from jax import lax
from jax.experimental import pallas as pl
from jax.experimental.pallas import tpu as pltpu

F32 = jnp.float32
BF16 = jnp.bfloat16

D_MODEL = 4096
GROUP_W = 1024
N_IN_SLICES = 12
SGU_CHUNK = 128
SGU_HEADS = 8
CONV_WIDTH = 31
CONV_GROUPS = 8
RET_HEADS = 4
RET_HEAD_DIM = 256
MEM_LEN = 256
MEM_HEADS = 4
MEM_HEAD_DIM = 256
ROPE_BASE = 10000.0
EPS = 1e-6

V7X_VMEM_BYTES = 64 * 1024 * 1024
LANES = 128

NORM_ROWS = 256
IN_BM = 1024
IN_BN = 512
OUT_BM = 512
OUT_BN = 1024
OUT_NJ = D_MODEL // OUT_BN
CONV_QR = OUT_BM // OUT_NJ
CONV_RC = 64
CONV_HALO = 32
MIX_ROWS = 256
RET_CHUNK = 256
MM_VMEM_LIMIT = 58 * 1024 * 1024
MIX_VMEM_LIMIT = 48 * 1024 * 1024


def _sigmoid(x):
    return 1.0 / (1.0 + jnp.exp(-x))


def _silu(x):
    return x * _sigmoid(x)


def _params(semantics, vmem_limit):
    return pltpu.CompilerParams(dimension_semantics=semantics, vmem_limit_bytes=vmem_limit)


def _rmsnorm_kernel(x_ref, g_ref, o_ref):
    x = x_ref[...]
    ms = jnp.mean(x * x, axis=-1, keepdims=True)
    o_ref[...] = ((x * lax.rsqrt(ms + EPS)) * g_ref[...]).astype(o_ref.dtype)


def _rmsnorm(x2d, g, out_dtype):
    rows, d = x2d.shape
    return pl.pallas_call(
        _rmsnorm_kernel,
        grid=(rows // NORM_ROWS,),
        in_specs=[pl.BlockSpec((NORM_ROWS, d), lambda i: (i, 0)),
                  pl.BlockSpec((1, d), lambda i: (0, 0))],
        out_specs=pl.BlockSpec((NORM_ROWS, d), lambda i: (i, 0)),
        out_shape=jax.ShapeDtypeStruct((rows, d), out_dtype),
        compiler_params=_params(("parallel",), MIX_VMEM_LIMIT),
        name="rmsnorm",
    )(x2d, g.reshape(1, d))


def _inproj_kernel(a_ref, w_ref, o_ref, wb_ref):
    @pl.when(pl.program_id(1) == 0)
    def _():
        wb_ref[...] = w_ref[...].astype(BF16)

    o_ref[...] = jnp.dot(a_ref[...], wb_ref[...], preferred_element_type=F32).astype(o_ref.dtype)


def _inproj(a, w_stack, layer, out_dtype):
    m, k = a.shape
    n = w_stack.shape[2]
    bm = min(IN_BM, m)
    return pl.pallas_call(
        _inproj_kernel,
        grid=(n // IN_BN, m // bm),
        in_specs=[pl.BlockSpec((bm, k), lambda j, i: (i, 0)),
                  pl.BlockSpec((None, k, IN_BN), lambda j, i: (layer, 0, j))],
        out_specs=pl.BlockSpec((bm, IN_BN), lambda j, i: (i, j)),
        out_shape=jax.ShapeDtypeStruct((m, n), out_dtype),
        scratch_shapes=[pltpu.VMEM((k, IN_BN), BF16)],
        compiler_params=_params(("arbitrary", "arbitrary"), MM_VMEM_LIMIT),
        name="inproj",
    )(a, w_stack)


def _rope_kernel(pos_ref, freq_ref, cos_ref, sin_ref):
    ang = pos_ref[...] * freq_ref[...]
    cos_ref[...] = jnp.cos(ang)
    sin_ref[...] = jnp.sin(ang)


def _rope_tables(pos_col, inv_freq):
    rows = pos_col.shape[0]
    half = inv_freq.shape[1]
    spec = pl.BlockSpec((MIX_ROWS, half), lambda i: (i, 0))
    return pl.pallas_call(
        _rope_kernel,
        grid=(rows // MIX_ROWS,),
        in_specs=[pl.BlockSpec((MIX_ROWS, 1), lambda i: (i, 0)),
                  pl.BlockSpec((1, half), lambda i: (0, 0))],
        out_specs=[spec, spec],
        out_shape=[jax.ShapeDtypeStruct((rows, half), F32)] * 2,
        compiler_params=_params(("parallel",), MIX_VMEM_LIMIT),
        name="rope_tables",
    )(pos_col, inv_freq)


def _slice_spec(rows, col_block):
    return pl.BlockSpec((rows, GROUP_W), lambda i: (i, col_block))


def _row_spec(n):
    return pl.BlockSpec((1, n), lambda i: (0, 0))


def _sgu_kernel(u_ref, v_ref, g_ref, ng_ref, nb_ref, w_ref, bias_ref, o_ref, vn_ref):
    v = v_ref[...].astype(F32)
    mu = jnp.mean(v, axis=-1, keepdims=True)
    var = jnp.mean(jnp.square(v - mu), axis=-1, keepdims=True)
    vn = ((v - mu) * lax.rsqrt(var + EPS)) * ng_ref[...] + nb_ref[...]
    vn_ref[...] = vn.astype(BF16)

    t = lax.broadcasted_iota(jnp.int32, (SGU_CHUNK, SGU_CHUNK), 0)
    s = lax.broadcasted_iota(jnp.int32, (SGU_CHUNK, SGU_CHUNK), 1)
    causal = t >= s
    for h in range(SGU_HEADS):
        w = jnp.where(causal, w_ref[h], 0.0).astype(BF16)
        cols = slice(h * SGU_CHUNK, (h + 1) * SGU_CHUNK)
        for n in range(MIX_ROWS // SGU_CHUNK):
            rows = slice(n * SGU_CHUNK, (n + 1) * SGU_CHUNK)
            mix = jnp.dot(w, vn_ref[rows, cols], preferred_element_type=F32) + bias_ref[:, cols]
            u = u_ref[rows, cols].astype(F32)
            gate = g_ref[rows, cols].astype(F32)
            o_ref[rows, cols] = ((u * mix) * _silu(gate)).astype(o_ref.dtype)


def _sgu(proj, ng, nb, w, bias_full):
    rows = proj.shape[0]
    return pl.pallas_call(
        _sgu_kernel,
        grid=(rows // MIX_ROWS,),
        in_specs=[_slice_spec(MIX_ROWS, 0), _slice_spec(MIX_ROWS, 1), _slice_spec(MIX_ROWS, 2),
                  _row_spec(GROUP_W), _row_spec(GROUP_W),
                  pl.BlockSpec((SGU_HEADS, SGU_CHUNK, SGU_CHUNK), lambda i: (0, 0, 0)),
                  pl.BlockSpec((SGU_CHUNK, GROUP_W), lambda i: (0, 0))],
        out_specs=pl.BlockSpec((MIX_ROWS, GROUP_W), lambda i: (i, 0)),
        out_shape=jax.ShapeDtypeStruct((rows, GROUP_W), BF16),
        scratch_shapes=[pltpu.VMEM((MIX_ROWS, GROUP_W), BF16)],
        compiler_params=_params(("parallel",), MIX_VMEM_LIMIT),
        name="sgu",
    )(proj, proj, proj, ng.reshape(1, -1), nb.reshape(1, -1), w, bias_full)


def _retention_kernel(steps_per_seq, logg_ref, q_ref, k_ref, v_ref, g_ref, cos_ref, sin_ref,
                      ng_ref, nb_ref, o_ref, state_ref):
    i = pl.program_id(0)

    @pl.when((i % steps_per_seq) == 0)
    def _():
        state_ref[...] = jnp.zeros_like(state_ref)

    c = RET_CHUNK
    half = RET_HEAD_DIM // 2
    cos = cos_ref[...]
    sin = sin_ref[...]
    t_row = lax.broadcasted_iota(jnp.int32, (c, c), 0)
    s_col = lax.broadcasted_iota(jnp.int32, (c, c), 1)
    diff = (t_row - s_col).astype(F32)
    t_vec = lax.broadcasted_iota(jnp.int32, (c, 1), 0).astype(F32)

    def rot(ref, lo):
        x1 = ref[:, lo:lo + half].astype(F32)
        x2 = ref[:, lo + half:lo + 2 * half].astype(F32)
        return x1 * cos - x2 * sin, x2 * cos + x1 * sin

    for h in range(RET_HEADS):
        lo = h * RET_HEAD_DIM
        log_g = logg_ref[h]
        q1, q2 = rot(q_ref, lo)
        k1, k2 = rot(k_ref, lo)
        scale = RET_HEAD_DIM ** -0.5
        q = jnp.concatenate([q1, q2], axis=-1)
        k = jnp.concatenate([k1, k2], axis=-1) * scale
        v = v_ref[:, lo:lo + RET_HEAD_DIM].astype(BF16)

        intra_decay = jnp.where(diff >= 0, jnp.exp(log_g * jnp.maximum(diff, 0.0)), 0.0)
        k_decay = jnp.exp(log_g * ((c - 1.0) - t_vec))
        q_decay = jnp.exp(log_g * (t_vec + 1.0))
        chunk_decay = jnp.exp(log_g * float(c))

        scores = lax.dot_general(q.astype(BF16), k.astype(BF16), (((1,), (1,)), ((), ())),
                                 preferred_element_type=F32) * intra_decay
        intra = jnp.dot(scores.astype(BF16), v, preferred_element_type=F32)
        state = state_ref[h]
        cross = jnp.dot((q * q_decay).astype(BF16), state.astype(BF16), preferred_element_type=F32)
        kv = lax.dot_general((k * k_decay).astype(BF16), v, (((0,), (0,)), ((), ())),
                             preferred_element_type=F32)
        state_ref[h] = chunk_decay * state + kv

        r = intra + cross
        mu = jnp.mean(r, axis=-1, keepdims=True)
        var = jnp.mean(jnp.square(r - mu), axis=-1, keepdims=True)
        cols = slice(lo, lo + RET_HEAD_DIM)
        y = ((r - mu) * lax.rsqrt(var + EPS)) * ng_ref[:, cols] + nb_ref[:, cols]
        gate = g_ref[:, cols].astype(F32)
        o_ref[:, cols] = (y * _silu(gate)).astype(o_ref.dtype)


def _retention(proj, seq_len, log_g, cos, sin, ng, nb):
    rows = proj.shape[0]
    half = RET_HEAD_DIM // 2
    kernel = functools.partial(_retention_kernel, seq_len // RET_CHUNK)
    tab_spec = pl.BlockSpec((RET_CHUNK, half), lambda i: (i, 0))
    return pl.pallas_call(
        kernel,
        grid=(rows // RET_CHUNK,),
        in_specs=[pl.BlockSpec(memory_space=pltpu.SMEM),
                  _slice_spec(RET_CHUNK, 6), _slice_spec(RET_CHUNK, 7), _slice_spec(RET_CHUNK, 8),
                  _slice_spec(RET_CHUNK, 9), tab_spec, tab_spec,
                  _row_spec(GROUP_W), _row_spec(GROUP_W)],
        out_specs=pl.BlockSpec((RET_CHUNK, GROUP_W), lambda i: (i, 0)),
        out_shape=jax.ShapeDtypeStruct((rows, GROUP_W), BF16),
        scratch_shapes=[pltpu.VMEM((RET_HEADS, RET_HEAD_DIM, RET_HEAD_DIM), F32)],
        compiler_params=_params(("arbitrary",), MIX_VMEM_LIMIT),
        name="retention",
    )(log_g, proj, proj, proj, proj, cos, sin, ng.reshape(1, -1), nb.reshape(1, -1))


def _memattn_kernel(q_ref, g_ref, k_ref, v_ref, o_ref):
    scale = MEM_HEAD_DIM ** -0.5
    for h in range(MEM_HEADS):
        cols = slice(h * MEM_HEAD_DIM, (h + 1) * MEM_HEAD_DIM)
        s = lax.dot_general(q_ref[:, cols], k_ref[:, cols], (((1,), (1,)), ((), ())),
                            preferred_element_type=F32) * scale
        m = jnp.max(s, axis=-1, keepdims=True)
        p = jnp.exp(s - m)
        denom = jnp.sum(p, axis=-1, keepdims=True)
        p = p / denom
        o = jnp.dot(p.astype(BF16), v_ref[:, cols], preferred_element_type=F32)
        gate = g_ref[:, cols].astype(F32)
        o_ref[:, cols] = (o * _silu(gate)).astype(o_ref.dtype)


def _memattn(proj, kv, seq_len):
    rows = proj.shape[0]
    steps_per_seq = seq_len // MIX_ROWS
    return pl.pallas_call(
        _memattn_kernel,
        grid=(rows // MIX_ROWS,),
        in_specs=[_slice_spec(MIX_ROWS, 10), _slice_spec(MIX_ROWS, 11),
                  pl.BlockSpec((MEM_LEN, GROUP_W), lambda i: (i // steps_per_seq, 0)),
                  pl.BlockSpec((MEM_LEN, GROUP_W), lambda i: (i // steps_per_seq, 1))],
        out_specs=pl.BlockSpec((MIX_ROWS, GROUP_W), lambda i: (i, 0)),
        out_shape=jax.ShapeDtypeStruct((rows, GROUP_W), BF16),
        compiler_params=_params(("parallel",), MIX_VMEM_LIMIT),
        name="memattn",
    )(proj, proj, kv, kv)


def _rows2(start, n):
    return pl.ds(2 * start, n, stride=2)


def _conv_rows(seq_start, row0, unrolled, ca_ref, cb_ref, cg_ref, cw_ref, cbias_ref, cng_ref,
               cnb_ref, dst_ref, buf_ref):
    n_slabs = GROUP_W // LANES
    for s in range(n_slabs):
        cols = slice(s * LANES, (s + 1) * LANES)
        tail = buf_ref[s, _rows2(CONV_QR, CONV_HALO), :]
        buf_ref[s, _rows2(0, CONV_HALO), :] = jnp.where(seq_start, 0.0, tail)
        buf_ref[s, _rows2(CONV_HALO, CONV_QR), :] = (
            ca_ref[:, cols].astype(F32) * _sigmoid(cb_ref[:, cols].astype(F32)))

    def chunk(c):
        r0 = c * CONV_RC
        for s in range(n_slabs):
            cols = slice(s * LANES, (s + 1) * LANES)
            acc = jnp.broadcast_to(cbias_ref[:, cols], (CONV_RC, LANES))
            for k in range(CONV_WIDTH):
                start = r0 + (CONV_HALO - (CONV_WIDTH - 1) + k)
                acc = acc + cw_ref[k:k + 1, cols] * buf_ref[s, _rows2(start, CONV_RC), :]
            mu = jnp.mean(acc, axis=-1, keepdims=True)
            var = jnp.mean(jnp.square(acc - mu), axis=-1, keepdims=True)
            y = ((acc - mu) * lax.rsqrt(var + EPS)) * cng_ref[:, cols] + cnb_ref[:, cols]
            if unrolled:
                gate = cg_ref[r0:r0 + CONV_RC, cols].astype(F32)
            else:
                gate = cg_ref[pl.ds(pl.multiple_of(r0, CONV_RC), CONV_RC), cols].astype(F32)
            out_rows = pl.ds(pl.multiple_of(row0 + r0, CONV_RC), CONV_RC)
            dst_ref[out_rows, cols] = (_silu(y) * _silu(gate)).astype(dst_ref.dtype)

    if unrolled:
        for c in range(CONV_QR // CONV_RC):
            chunk(c)
    else:
        def body(c, carry):
            chunk(c)
            return carry
        lax.fori_loop(0, CONV_QR // CONV_RC, body, 0)


def _outproj_kernel(emit_x, steps_per_seq, ya_ref, yc_ref, ym_ref, w_ref, x_ref, ng_ref,
                    ca_ref, cb_ref, cg_ref, cw_ref, cbias_ref, cng_ref, cnb_ref, *rest):
    if emit_x:
        xo_ref, no_ref, yb_ref, buf_ref, ssq_ref, xrow_ref = rest
    else:
        xo_ref = None
        no_ref, yb_ref, buf_ref, ssq_ref = rest
        xrow_ref = no_ref
    i = pl.program_id(0)
    j = pl.program_id(1)
    step = i * OUT_NJ + j
    seq_start = (step % steps_per_seq) == 0
    conv_args = (ca_ref, cb_ref, cg_ref, cw_ref, cbias_ref, cng_ref, cnb_ref)
    row0 = j * CONV_QR

    @pl.when(i == 0)
    def _():
        @pl.when(j == 0)
        def _():
            for s in range(GROUP_W // LANES):
                buf_ref[s, _rows2(CONV_QR, CONV_HALO), :] = jnp.zeros((CONV_HALO, LANES), F32)

        _conv_rows(seq_start, row0, False, *conv_args, yb_ref.at[0], buf_ref)

    @pl.when(i > 0)
    def _():
        acc = x_ref[...]
        ys = ((1, yb_ref[(i - 1) % 2]), (0, ya_ref[...]), (2, yc_ref[...]), (3, ym_ref[...]))
        for g, y in ys:
            acc = acc + jnp.dot(y, w_ref[g * GROUP_W:(g + 1) * GROUP_W, :],
                                preferred_element_type=F32)
        if emit_x:
            xo_ref[...] = acc
        col0 = pl.multiple_of(j * OUT_BN, OUT_BN)
        xrow_ref[:, pl.ds(col0, OUT_BN)] = acc
        ssq_ref[j] = jnp.sum(acc * acc, axis=-1, keepdims=True)

        _conv_rows(seq_start, row0, True, *conv_args, yb_ref.at[i % 2], buf_ref)

        @pl.when(j == OUT_NJ - 1)
        def _():
            ssq = ssq_ref[0]
            for jj in range(1, OUT_NJ):
                ssq = ssq + ssq_ref[jj]
            rstd = lax.rsqrt(ssq * (1.0 / D_MODEL) + EPS)
            for jj in range(OUT_NJ):
                cols = slice(jj * OUT_BN, (jj + 1) * OUT_BN)
                no_ref[:, cols] = ((xrow_ref[:, cols] * rstd) * ng_ref[:, cols]).astype(no_ref.dtype)


def _outproj(ys, proj, seq_len, w_stack, layer, x2d, next_g, conv_w, conv_b, conv_ng, conv_nb,
             emit_x, norm_dtype):
    m, d = x2d.shape
    k = w_stack.shape[1]
    nb = m // OUT_BM
    steps_per_seq = seq_len // CONV_QR
    last_q = m // CONV_QR - 1

    def prev(i):
        return jnp.maximum(i - 1, 0)

    y_spec = pl.BlockSpec((OUT_BM, GROUP_W), lambda i, j: (prev(i), 0))

    def conv_spec(col):
        return pl.BlockSpec((CONV_QR, GROUP_W),
                            lambda i, j: (jnp.minimum(i * OUT_NJ + j, last_q), col))

    def tile_idx(i, j):
        return (prev(i), jnp.where(i == 0, 0, j))

    row_full = pl.BlockSpec((1, d), lambda i, j: (0, 0))
    row_grp = pl.BlockSpec((1, GROUP_W), lambda i, j: (0, 0))
    in_specs = [y_spec, y_spec, y_spec,
                pl.BlockSpec((None, k, OUT_BN), lambda i, j: (layer, 0, j)),
                pl.BlockSpec((OUT_BM, OUT_BN), tile_idx),
                row_full,
                conv_spec(3), conv_spec(4), conv_spec(5),
                pl.BlockSpec((CONV_WIDTH, GROUP_W), lambda i, j: (0, 0)),
                row_grp, row_grp, row_grp]
    norm_spec = pl.BlockSpec((OUT_BM, d), lambda i, j: (prev(i), 0))
    norm_shape = jax.ShapeDtypeStruct((m, d), norm_dtype)
    if emit_x:
        out_specs = [pl.BlockSpec((OUT_BM, OUT_BN), tile_idx), norm_spec]
        out_shape = [jax.ShapeDtypeStruct((m, d), F32), norm_shape]
    else:
        out_specs = [norm_spec]
        out_shape = [norm_shape]
    scratch = [pltpu.VMEM((2, OUT_BM, GROUP_W), BF16),
               pltpu.VMEM((GROUP_W // LANES, 2 * (CONV_HALO + CONV_QR), LANES), F32),
               pltpu.VMEM((OUT_NJ, OUT_BM, 1), F32)]
    if emit_x:
        scratch.append(pltpu.VMEM((OUT_BM, d), F32))
    else:
        assert norm_dtype == F32
    ya, yc, ym = ys
    outs = pl.pallas_call(
        functools.partial(_outproj_kernel, emit_x, steps_per_seq),
        grid=(nb + 1, OUT_NJ),
        in_specs=in_specs,
        out_specs=out_specs,
        out_shape=out_shape,
        scratch_shapes=scratch,
        compiler_params=_params(("arbitrary", "arbitrary"), MM_VMEM_LIMIT),
        name="outproj",
    )(ya, yc, ym, w_stack, x2d, next_g.reshape(1, d), proj, proj, proj,
      conv_w, conv_b.reshape(1, -1), conv_ng.reshape(1, -1), conv_nb.reshape(1, -1))
    return outs if emit_x else (None, outs[0])


def kernel(x, mem, positions, norm_g, w_in, sgu_norm_g, sgu_norm_b, sgu_w, sgu_b,
           conv_w, conv_b, conv_norm_g, conv_norm_b, ret_norm_g, ret_norm_b,
           mem_norm_g, w_mem_kv, w_out, final_norm_g):
    bsz, s_len, d = x.shape
    depth = w_in.shape[0]
    assert d == D_MODEL and w_in.shape[2] == N_IN_SLICES * GROUP_W
    assert s_len % OUT_BM == 0 and s_len % RET_CHUNK == 0 and MIX_ROWS % SGU_CHUNK == 0
    assert GROUP_W // CONV_GROUPS == LANES and CONV_HALO >= CONV_WIDTH - 1
    assert CONV_QR % CONV_RC == 0 and CONV_QR >= CONV_HALO
    rows = bsz * s_len
    x2d = x.reshape(rows, d)
    mem2d = mem.reshape(bsz * MEM_LEN, d)

    half = RET_HEAD_DIM // 2
    inv_freq = jnp.power(ROPE_BASE, -jnp.arange(half, dtype=F32) / half).reshape(1, half)
    pos_col = positions.astype(F32).reshape(rows, 1)
    cos, sin = _rope_tables(pos_col, inv_freq)
    log_g = jnp.log(1.0 - jnp.power(2.0, -5.0 - jnp.arange(RET_HEADS, dtype=F32)))
    w_out_bf = w_out.astype(BF16)

    h = _rmsnorm(x2d, norm_g[0], BF16)
    out = None
    for l in range(depth):
        last = l == depth - 1
        proj = _inproj(h, w_in, l, BF16)

        bias_full = jnp.repeat(sgu_b[l].T, GROUP_W // SGU_HEADS, axis=1)
        y_a = _sgu(proj, sgu_norm_g[l], sgu_norm_b[l], sgu_w[l], bias_full)
        y_c = _retention(proj, s_len, log_g, cos, sin, ret_norm_g[l], ret_norm_b[l])

        mem_n = _rmsnorm(mem2d, mem_norm_g[l], BF16)
        kv = _inproj(mem_n, w_mem_kv, l, BF16)
        y_m = _memattn(proj, kv, s_len)

        next_g = final_norm_g if last else norm_g[l + 1]
        x2d, normed = _outproj((y_a, y_c, y_m), proj, s_len, w_out_bf, l, x2d, next_g,
                               conv_w[l], conv_b[l], conv_norm_g[l], conv_norm_b[l],
                               emit_x=not last, norm_dtype=F32 if last else BF16)
        if last:
            out = normed
        else:
            h = normed
    return out.reshape(bsz, s_len, d)
```

```python
import functools

import jax
import jax.numpy as jnp
from jax import lax
from jax.experimental import pallas as pl
from jax.experimental.pallas import tpu as pltpu

F32 = jnp.float32
BF16 = jnp.bfloat16

D_MODEL = 4096
GROUP_W = 1024
N_IN_SLICES = 12
SGU_CHUNK = 128
SGU_HEADS = 8
CONV_WIDTH = 31
CONV_GROUPS = 8
RET_HEADS = 4
RET_HEAD_DIM = 256
MEM_LEN = 256
MEM_HEADS = 4
MEM_HEAD_DIM = 256
ROPE_BASE = 10000.0
EPS = 1e-6

V7X_VMEM_BYTES = 64 * 1024 * 1024
LANES = 128

NORM_ROWS = 256
IN_BM = 1024
IN_BN = 1024
IN_MIN_ROW_TILES = 4
IN_STAGE_BYTES = 8 * 1024 * 1024
OUT_BM = 512
OUT_BN = 1024
OUT_NJ = D_MODEL // OUT_BN
CONV_QR = OUT_BM // OUT_NJ
CONV_RC = 64
CONV_HALO = 32
MIX_ROWS = 512
RET_CHUNK = 256
MM_VMEM_LIMIT = 58 * 1024 * 1024
MIX_VMEM_LIMIT = 48 * 1024 * 1024


def _sigmoid(x):
    return 1.0 / (1.0 + jnp.exp(-x))


def _silu(x):
    return x * _sigmoid(x)


def _params(semantics, vmem_limit):
    return pltpu.CompilerParams(dimension_semantics=semantics, vmem_limit_bytes=vmem_limit)


def _rmsnorm_kernel(x_ref, g_ref, o_ref):
    x = x_ref[...]
    ms = jnp.mean(x * x, axis=-1, keepdims=True)
    o_ref[...] = ((x * lax.rsqrt(ms + EPS)) * g_ref[...]).astype(o_ref.dtype)


def _rmsnorm(x2d, g, out_dtype):
    rows, d = x2d.shape
    return pl.pallas_call(
        _rmsnorm_kernel,
        grid=(rows // NORM_ROWS,),
        in_specs=[pl.BlockSpec((NORM_ROWS, d), lambda i: (i, 0)),
                  pl.BlockSpec((1, d), lambda i: (0, 0))],
        out_specs=pl.BlockSpec((NORM_ROWS, d), lambda i: (i, 0)),
        out_shape=jax.ShapeDtypeStruct((rows, d), out_dtype),
        compiler_params=_params(("parallel",), MIX_VMEM_LIMIT),
        name="rmsnorm",
    )(x2d, g.reshape(1, d))


def _inproj_kernel(layer, n_col_blocks, n_row_tiles, a_ref, w_hbm, o_ref, wb_ref, stage_ref, sem):
    j = pl.program_id(0)
    i = pl.program_id(1)
    n_slots, kc, _ = stage_ref.shape

    def chunk_copy(col_block, c, slot):
        rows = pl.ds(pl.multiple_of(c * kc, kc), kc)
        cols = pl.ds(pl.multiple_of(col_block * IN_BN, IN_BN), IN_BN)
        return pltpu.make_async_copy(w_hbm.at[layer, rows, cols], stage_ref.at[slot], sem.at[slot])

    @pl.when((j == 0) & (i == 0))
    def _():
        copies = [chunk_copy(0, c, c % n_slots) for c in range(n_row_tiles)]
        for c in range(min(n_slots, n_row_tiles)):
            copies[c].start()
        for c in range(n_row_tiles):
            copies[c].wait()
            wb_ref[0, c * kc:(c + 1) * kc, :] = stage_ref[c % n_slots].astype(BF16)
            if c + n_slots < n_row_tiles:
                copies[c + n_slots].start()

    has_next = j + 1 < n_col_blocks

    @pl.when(has_next)
    def _():
        chunk_copy(j + 1, i, 0).start()

    o_ref[...] = jnp.dot(a_ref[...], wb_ref[j % 2], preferred_element_type=F32).astype(o_ref.dtype)

    @pl.when(has_next)
    def _():
        chunk_copy(j + 1, i, 0).wait()
        rows = pl.ds(pl.multiple_of(i * kc, kc), kc)
        wb_ref[(j + 1) % 2, rows, :] = stage_ref[0].astype(BF16)


def _inproj(a, w_stack, layer, out_dtype):
    m, k = a.shape
    n = w_stack.shape[2]
    bm = min(IN_BM, m // IN_MIN_ROW_TILES)
    n_row_tiles = m // bm
    kc = k // n_row_tiles
    n_slots = max(2, min(4, IN_STAGE_BYTES // (kc * IN_BN * 4)))
    kernel = functools.partial(_inproj_kernel, layer, n // IN_BN, n_row_tiles)
    return pl.pallas_call(
        kernel,
        grid=(n // IN_BN, n_row_tiles),
        in_specs=[pl.BlockSpec((bm, k), lambda j, i: (i, 0)),
                  pl.BlockSpec(memory_space=pl.ANY)],
        out_specs=pl.BlockSpec((bm, IN_BN), lambda j, i: (i, j)),
        out_shape=jax.ShapeDtypeStruct((m, n), out_dtype),
        scratch_shapes=[pltpu.VMEM((2, k, IN_BN), BF16),
                        pltpu.VMEM((n_slots, kc, IN_BN), F32),
                        pltpu.SemaphoreType.DMA((n_slots,))],
        compiler_params=_params(("arbitrary", "arbitrary"), MM_VMEM_LIMIT),
        name="inproj",
    )(a, w_stack)


def _rope_kernel(pos_ref, freq_ref, cos_ref, sin_ref):
    ang = pos_ref[...] * freq_ref[...]
    cos_ref[...] = jnp.cos(ang)
    sin_ref[...] = jnp.sin(ang)


def _rope_tables(pos_col, inv_freq):
    rows = pos_col.shape[0]
    half = inv_freq.shape[1]
    spec = pl.BlockSpec((MIX_ROWS, half), lambda i: (i, 0))
    return pl.pallas_call(
        _rope_kernel,
        grid=(rows // MIX_ROWS,),
        in_specs=[pl.BlockSpec((MIX_ROWS, 1), lambda i: (i, 0)),
                  pl.BlockSpec((1, half), lambda i: (0, 0))],
        out_specs=[spec, spec],
        out_shape=[jax.ShapeDtypeStruct((rows, half), F32)] * 2,
        compiler_params=_params(("parallel",), MIX_VMEM_LIMIT),
        name="rope_tables",
    )(pos_col, inv_freq)


def _slice_spec(rows, col_block):
    return pl.BlockSpec((rows, GROUP_W), lambda i: (i, col_block))


def _row_spec(n):
    return pl.BlockSpec((1, n), lambda i: (0, 0))


def _sgu_kernel(u_ref, v_ref, g_ref, ng_ref, nb_ref, w_ref, bias_ref, o_ref, vn_ref):
    v = v_ref[...].astype(F32)
    mu = jnp.mean(v, axis=-1, keepdims=True)
    var = jnp.mean(jnp.square(v - mu), axis=-1, keepdims=True)
    vn = ((v - mu) * lax.rsqrt(var + EPS)) * ng_ref[...] + nb_ref[...]
    vn_ref[...] = vn.astype(BF16)

    t = lax.broadcasted_iota(jnp.int32, (SGU_CHUNK, SGU_CHUNK), 0)
    s = lax.broadcasted_iota(jnp.int32, (SGU_CHUNK, SGU_CHUNK), 1)
    causal = t >= s
    for h in range(SGU_HEADS):
        w = jnp.where(causal, w_ref[h], 0.0).astype(BF16)
        cols = slice(h * SGU_CHUNK, (h + 1) * SGU_CHUNK)
        for n in range(MIX_ROWS // SGU_CHUNK):
            rows = slice(n * SGU_CHUNK, (n + 1) * SGU_CHUNK)
            mix = jnp.dot(w, vn_ref[rows, cols], preferred_element_type=F32) + bias_ref[:, cols]
            u = u_ref[rows, cols].astype(F32)
            gate = g_ref[rows, cols].astype(F32)
            o_ref[rows, cols] = ((u * mix) * _silu(gate)).astype(o_ref.dtype)


def _sgu(proj, ng, nb, w, bias_full):
    rows = proj.shape[0]
    return pl.pallas_call(
        _sgu_kernel,
        grid=(rows // MIX_ROWS,),
        in_specs=[_slice_spec(MIX_ROWS, 0), _slice_spec(MIX_ROWS, 1), _slice_spec(MIX_ROWS, 2),
                  _row_spec(GROUP_W), _row_spec(GROUP_W),
                  pl.BlockSpec((SGU_HEADS, SGU_CHUNK, SGU_CHUNK), lambda i: (0, 0, 0)),
                  pl.BlockSpec((SGU_CHUNK, GROUP_W), lambda i: (0, 0))],
        out_specs=pl.BlockSpec((MIX_ROWS, GROUP_W), lambda i: (i, 0)),
        out_shape=jax.ShapeDtypeStruct((rows, GROUP_W), BF16),
        scratch_shapes=[pltpu.VMEM((MIX_ROWS, GROUP_W), BF16)],
        compiler_params=_params(("parallel",), MIX_VMEM_LIMIT),
        name="sgu",
    )(proj, proj, proj, ng.reshape(1, -1), nb.reshape(1, -1), w, bias_full)


def _retention_kernel(steps_per_seq, logg_ref, q_ref, k_ref, v_ref, g_ref, cos_ref, sin_ref,
                      ng_ref, nb_ref, o_ref, state_ref, decay_ref):
    i = pl.program_id(0)

    @pl.when((i % steps_per_seq) == 0)
    def _():
        state_ref[...] = jnp.zeros_like(state_ref)

    c = RET_CHUNK
    half = RET_HEAD_DIM // 2
    scale = RET_HEAD_DIM ** -0.5
    t_row = lax.broadcasted_iota(jnp.int32, (c, c), 0)
    s_col = lax.broadcasted_iota(jnp.int32, (c, c), 1)
    diff = (t_row - s_col).astype(F32)
    t_vec = lax.broadcasted_iota(jnp.int32, (c, 1), 0).astype(F32)
    for h in range(RET_HEADS):
        log_g = logg_ref[h]
        decay_ref[h] = jnp.where(diff >= 0, jnp.exp(log_g * jnp.maximum(diff, 0.0)), 0.0)

    def chunk(ci, carry):
        rows = pl.ds(pl.multiple_of(ci * c, c), c)
        cos = cos_ref[rows, :]
        sin = sin_ref[rows, :]

        def rot(ref, lo):
            x1 = ref[rows, lo:lo + half].astype(F32)
            x2 = ref[rows, lo + half:lo + 2 * half].astype(F32)
            return x1 * cos - x2 * sin, x2 * cos + x1 * sin

        for h in range(RET_HEADS):
            lo = h * RET_HEAD_DIM
            cols = slice(lo, lo + RET_HEAD_DIM)
            log_g = logg_ref[h]
            q1, q2 = rot(q_ref, lo)
            k1, k2 = rot(k_ref, lo)
            q = jnp.concatenate([q1, q2], axis=-1)
            k = jnp.concatenate([k1, k2], axis=-1) * scale
            v = v_ref[rows, cols].astype(BF16)

            k_decay = jnp.exp(log_g * ((c - 1.0) - t_vec))
            q_decay = jnp.exp(log_g * (t_vec + 1.0))
            chunk_decay = jnp.exp(log_g * float(c))

            scores = lax.dot_general(q.astype(BF16), k.astype(BF16), (((1,), (1,)), ((), ())),
                                     preferred_element_type=F32) * decay_ref[h]
            intra = jnp.dot(scores.astype(BF16), v, preferred_element_type=F32)
            state = state_ref[h]
            cross = jnp.dot((q * q_decay).astype(BF16), state.astype(BF16),
                            preferred_element_type=F32)
            kv = lax.dot_general((k * k_decay).astype(BF16), v, (((0,), (0,)), ((), ())),
                                 preferred_element_type=F32)
            state_ref[h] = chunk_decay * state + kv

            r = intra + cross
            mu = jnp.mean(r, axis=-1, keepdims=True)
            var = jnp.mean(jnp.square(r - mu), axis=-1, keepdims=True)
            y = ((r - mu) * lax.rsqrt(var + EPS)) * ng_ref[:, cols] + nb_ref[:, cols]
            gate = g_ref[rows, cols].astype(F32)
            o_ref[rows, cols] = (y * _silu(gate)).astype(o_ref.dtype)
        return carry

    lax.fori_loop(0, MIX_ROWS // c, chunk, 0)


def _retention(proj, seq_len, log_g, cos, sin, ng, nb):
    rows = proj.shape[0]
    half = RET_HEAD_DIM // 2
    kernel = functools.partial(_retention_kernel, seq_len // MIX_ROWS)
    tab_spec = pl.BlockSpec((MIX_ROWS, half), lambda i: (i, 0))
    return pl.pallas_call(
        kernel,
        grid=(rows // MIX_ROWS,),
        in_specs=[pl.BlockSpec(memory_space=pltpu.SMEM),
                  _slice_spec(MIX_ROWS, 6), _slice_spec(MIX_ROWS, 7), _slice_spec(MIX_ROWS, 8),
                  _slice_spec(MIX_ROWS, 9), tab_spec, tab_spec,
                  _row_spec(GROUP_W), _row_spec(GROUP_W)],
        out_specs=pl.BlockSpec((MIX_ROWS, GROUP_W), lambda i: (i, 0)),
        out_shape=jax.ShapeDtypeStruct((rows, GROUP_W), BF16),
        scratch_shapes=[pltpu.VMEM((RET_HEADS, RET_HEAD_DIM, RET_HEAD_DIM), F32),
                        pltpu.VMEM((RET_HEADS, RET_CHUNK, RET_CHUNK), F32)],
        compiler_params=_params(("arbitrary",), MIX_VMEM_LIMIT),
        name="retention",
    )(log_g, proj, proj, proj, proj, cos, sin, ng.reshape(1, -1), nb.reshape(1, -1))


def _memattn_kernel(q_ref, g_ref, k_ref, v_ref, o_ref):
    scale = MEM_HEAD_DIM ** -0.5
    for h in range(MEM_HEADS):
        cols = slice(h * MEM_HEAD_DIM, (h + 1) * MEM_HEAD_DIM)
        s = lax.dot_general(q_ref[:, cols], k_ref[:, cols], (((1,), (1,)), ((), ())),
                            preferred_element_type=F32) * scale
        m = jnp.max(s, axis=-1, keepdims=True)
        p = jnp.exp(s - m)
        denom = jnp.sum(p, axis=-1, keepdims=True)
        p = p / denom
        o = jnp.dot(p.astype(BF16), v_ref[:, cols], preferred_element_type=F32)
        gate = g_ref[:, cols].astype(F32)
        o_ref[:, cols] = (o * _silu(gate)).astype(o_ref.dtype)


def _memattn(proj, kv, seq_len):
    rows = proj.shape[0]
    steps_per_seq = seq_len // MIX_ROWS
    return pl.pallas_call(
        _memattn_kernel,
        grid=(rows // MIX_ROWS,),
        in_specs=[_slice_spec(MIX_ROWS, 10), _slice_spec(MIX_ROWS, 11),
                  pl.BlockSpec((MEM_LEN, GROUP_W), lambda i: (i // steps_per_seq, 0)),
                  pl.BlockSpec((MEM_LEN, GROUP_W), lambda i: (i // steps_per_seq, 1))],
        out_specs=pl.BlockSpec((MIX_ROWS, GROUP_W), lambda i: (i, 0)),
        out_shape=jax.ShapeDtypeStruct((rows, GROUP_W), BF16),
        compiler_params=_params(("parallel",), MIX_VMEM_LIMIT),
        name="memattn",
    )(proj, proj, kv, kv)


def _rows2(start, n):
    return pl.ds(2 * start, n, stride=2)


def _conv_rows(seq_start, row0, unrolled, ca_ref, cb_ref, cg_ref, cw_ref, cbias_ref, cng_ref,
               cnb_ref, dst_ref, buf_ref):
    n_slabs = GROUP_W // LANES
    for s in range(n_slabs):
        cols = slice(s * LANES, (s + 1) * LANES)
        tail = buf_ref[s, _rows2(CONV_QR, CONV_HALO), :]
        buf_ref[s, _rows2(0, CONV_HALO), :] = jnp.where(seq_start, 0.0, tail)
        buf_ref[s, _rows2(CONV_HALO, CONV_QR), :] = (
            ca_ref[:, cols].astype(F32) * _sigmoid(cb_ref[:, cols].astype(F32)))

    def chunk(c):
        r0 = c * CONV_RC
        for s in range(n_slabs):
            cols = slice(s * LANES, (s + 1) * LANES)
            acc = jnp.broadcast_to(cbias_ref[:, cols], (CONV_RC, LANES))
            for k in range(CONV_WIDTH):
                start = r0 + (CONV_HALO - (CONV_WIDTH - 1) + k)
                acc = acc + cw_ref[k:k + 1, cols] * buf_ref[s, _rows2(start, CONV_RC), :]
            mu = jnp.mean(acc, axis=-1, keepdims=True)
            var = jnp.mean(jnp.square(acc - mu), axis=-1, keepdims=True)
            y = ((acc - mu) * lax.rsqrt(var + EPS)) * cng_ref[:, cols] + cnb_ref[:, cols]
            if unrolled:
                gate = cg_ref[r0:r0 + CONV_RC, cols].astype(F32)
            else:
                gate = cg_ref[pl.ds(pl.multiple_of(r0, CONV_RC), CONV_RC), cols].astype(F32)
            out_rows = pl.ds(pl.multiple_of(row0 + r0, CONV_RC), CONV_RC)
            dst_ref[out_rows, cols] = (_silu(y) * _silu(gate)).astype(dst_ref.dtype)

    if unrolled:
        for c in range(CONV_QR // CONV_RC):
            chunk(c)
    else:
        def body(c, carry):
            chunk(c)
            return carry
        lax.fori_loop(0, CONV_QR // CONV_RC, body, 0)


def _outproj_kernel(emit_x, steps_per_seq, ya_ref, yc_ref, ym_ref, w_ref, x_ref, ng_ref,
                    ca_ref, cb_ref, cg_ref, cw_ref, cbias_ref, cng_ref, cnb_ref, *rest):
    if emit_x:
        xo_ref, no_ref, yb_ref, buf_ref, ssq_ref, xrow_ref = rest
    else:
        xo_ref = None
        no_ref, yb_ref, buf_ref, ssq_ref = rest
        xrow_ref = no_ref
    i = pl.program_id(0)
    j = pl.program_id(1)
    step = i * OUT_NJ + j
    seq_start = (step % steps_per_seq) == 0
    conv_args = (ca_ref, cb_ref, cg_ref, cw_ref, cbias_ref, cng_ref, cnb_ref)
    row0 = j * CONV_QR

    @pl.when(i == 0)
    def _():
        @pl.when(j == 0)
        def _():
            for s in range(GROUP_W // LANES):
                buf_ref[s, _rows2(CONV_QR, CONV_HALO), :] = jnp.zeros((CONV_HALO, LANES), F32)

        _conv_rows(seq_start, row0, False, *conv_args, yb_ref.at[0], buf_ref)

    @pl.when(i > 0)
    def _():
        acc = x_ref[...]
        ys = ((1, yb_ref[(i - 1) % 2]), (0, ya_ref[...]), (2, yc_ref[...]), (3, ym_ref[...]))
        for g, y in ys:
            acc = acc + jnp.dot(y, w_ref[g * GROUP_W:(g + 1) * GROUP_W, :],
                                preferred_element_type=F32)
        if emit_x:
            xo_ref[...] = acc
        col0 = pl.multiple_of(j * OUT_BN, OUT_BN)
        xrow_ref[:, pl.ds(col0, OUT_BN)] = acc
        ssq_ref[j] = jnp.sum(acc * acc, axis=-1, keepdims=True)

        _conv_rows(seq_start, row0, True, *conv_args, yb_ref.at[i % 2], buf_ref)

        @pl.when(j == OUT_NJ - 1)
        def _():
            ssq = ssq_ref[0]
            for jj in range(1, OUT_NJ):
                ssq = ssq + ssq_ref[jj]
            rstd = lax.rsqrt(ssq * (1.0 / D_MODEL) + EPS)
            for jj in range(OUT_NJ):
                cols = slice(jj * OUT_BN, (jj + 1) * OUT_BN)
                no_ref[:, cols] = ((xrow_ref[:, cols] * rstd) * ng_ref[:, cols]).astype(no_ref.dtype)


def _outproj(ys, proj, seq_len, w_stack, layer, x2d, next_g, conv_w, conv_b, conv_ng, conv_nb,
             emit_x, norm_dtype):
    m, d = x2d.shape
    k = w_stack.shape[1]
    nb = m // OUT_BM
    steps_per_seq = seq_len // CONV_QR
    last_q = m // CONV_QR - 1

    def prev(i):
        return jnp.maximum(i - 1, 0)

    y_spec = pl.BlockSpec((OUT_BM, GROUP_W), lambda i, j: (prev(i), 0))

    def conv_spec(col):
        return pl.BlockSpec((CONV_QR, GROUP_W),
                            lambda i, j: (jnp.minimum(i * OUT_NJ + j, last_q), col))

    def tile_idx(i, j):
        return (prev(i), jnp.where(i == 0, 0, j))

    row_full = pl.BlockSpec((1, d), lambda i, j: (0, 0))
    row_grp = pl.BlockSpec((1, GROUP_W), lambda i, j: (0, 0))
    in_specs = [y_spec, y_spec, y_spec,
                pl.BlockSpec((None, k, OUT_BN), lambda i, j: (layer, 0, j)),
                pl.BlockSpec((OUT_BM, OUT_BN), tile_idx),
                row_full,
                conv_spec(3), conv_spec(4), conv_spec(5),
                pl.BlockSpec((CONV_WIDTH, GROUP_W), lambda i, j: (0, 0)),
                row_grp, row_grp, row_grp]
    norm_spec = pl.BlockSpec((OUT_BM, d), lambda i, j: (prev(i), 0))
    norm_shape = jax.ShapeDtypeStruct((m, d), norm_dtype)
    if emit_x:
        out_specs = [pl.BlockSpec((OUT_BM, OUT_BN), tile_idx), norm_spec]
        out_shape = [jax.ShapeDtypeStruct((m, d), F32), norm_shape]
    else:
        out_specs = [norm_spec]
        out_shape = [norm_shape]
    scratch = [pltpu.VMEM((2, OUT_BM, GROUP_W), BF16),
               pltpu.VMEM((GROUP_W // LANES, 2 * (CONV_HALO + CONV_QR), LANES), F32),
               pltpu.VMEM((OUT_NJ, OUT_BM, 1), F32)]
    if emit_x:
        scratch.append(pltpu.VMEM((OUT_BM, d), F32))
    else:
        assert norm_dtype == F32
    ya, yc, ym = ys
    outs = pl.pallas_call(
        functools.partial(_outproj_kernel, emit_x, steps_per_seq),
        grid=(nb + 1, OUT_NJ),
        in_specs=in_specs,
        out_specs=out_specs,
        out_shape=out_shape,
        scratch_shapes=scratch,
        compiler_params=_params(("arbitrary", "arbitrary"), MM_VMEM_LIMIT),
        name="outproj",
    )(ya, yc, ym, w_stack, x2d, next_g.reshape(1, d), proj, proj, proj,
      conv_w, conv_b.reshape(1, -1), conv_ng.reshape(1, -1), conv_nb.reshape(1, -1))
    return outs if emit_x else (None, outs[0])


def kernel(x, mem, positions, norm_g, w_in, sgu_norm_g, sgu_norm_b, sgu_w, sgu_b,
           conv_w, conv_b, conv_norm_g, conv_norm_b, ret_norm_g, ret_norm_b,
           mem_norm_g, w_mem_kv, w_out, final_norm_g):
    bsz, s_len, d = x.shape
    depth = w_in.shape[0]
    assert d == D_MODEL and w_in.shape[2] == N_IN_SLICES * GROUP_W
    assert s_len % OUT_BM == 0 and s_len % MIX_ROWS == 0
    assert MIX_ROWS % SGU_CHUNK == 0 and MIX_ROWS % RET_CHUNK == 0
    assert GROUP_W // CONV_GROUPS == LANES and CONV_HALO >= CONV_WIDTH - 1
    assert CONV_QR % CONV_RC == 0 and CONV_QR >= CONV_HALO
    rows = bsz * s_len
    x2d = x.reshape(rows, d)
    mem2d = mem.reshape(bsz * MEM_LEN, d)

    half = RET_HEAD_DIM // 2
    inv_freq = jnp.power(ROPE_BASE, -jnp.arange(half, dtype=F32) / half).reshape(1, half)
    pos_col = positions.astype(F32).reshape(rows, 1)
    cos, sin = _rope_tables(pos_col, inv_freq)
    log_g = jnp.log(1.0 - jnp.power(2.0, -5.0 - jnp.arange(RET_HEADS, dtype=F32)))
    w_out_bf = w_out.astype(BF16)

    h = _rmsnorm(x2d, norm_g[0], BF16)
    out = None
    for l in range(depth):
        last = l == depth - 1
        proj = _inproj(h, w_in, l, BF16)

        bias_full = jnp.repeat(sgu_b[l].T, GROUP_W // SGU_HEADS, axis=1)
        y_a = _sgu(proj, sgu_norm_g[l], sgu_norm_b[l], sgu_w[l], bias_full)
        y_c = _retention(proj, s_len, log_g, cos, sin, ret_norm_g[l], ret_norm_b[l])

        mem_n = _rmsnorm(mem2d, mem_norm_g[l], BF16)
        kv = _inproj(mem_n, w_mem_kv, l, BF16)
        y_m = _memattn(proj, kv, s_len)

        next_g = final_norm_g if last else norm_g[l + 1]
        x2d, normed = _outproj((y_a, y_c, y_m), proj, s_len, w_out_bf, l, x2d, next_g,
                               conv_w[l], conv_b[l], conv_norm_g[l], conv_norm_b[l],
                               emit_x=not last, norm_dtype=F32 if last else BF16)
        if last:
            out = normed
        else:
            h = normed
    return out.reshape(bsz, s_len, d)
```

```python
import functools

import jax
import jax.numpy as jnp
from jax import lax
from jax.experimental import pallas as pl
from jax.experimental.pallas import tpu as pltpu

F32 = jnp.float32
BF16 = jnp.bfloat16

D_MODEL = 4096
GROUP_W = 1024
N_IN_SLICES = 12
SGU_CHUNK = 128
SGU_HEADS = 8
CONV_WIDTH = 31
CONV_GROUPS = 8
RET_HEADS = 4
RET_HEAD_DIM = 256
MEM_LEN = 256
MEM_HEADS = 4
MEM_HEAD_DIM = 256
ROPE_BASE = 10000.0
EPS = 1e-6

V7X_VMEM_BYTES = 64 * 1024 * 1024
LANES = 128

NORM_ROWS = 256
IN_BM = 1024
IN_BN = 1024
IN_MIN_ROW_TILES = 4
IN_STAGE_BYTES = 8 * 1024 * 1024
OUT_BM = 512
OUT_BN = 1024
OUT_NJ = D_MODEL // OUT_BN
CONV_QR = OUT_BM // OUT_NJ
CONV_RC = 16
CONV_HALO = 32
MIX_ROWS = 512
RET_CHUNK = 256
MM_VMEM_LIMIT = 58 * 1024 * 1024
MIX_VMEM_LIMIT = 48 * 1024 * 1024


def _sigmoid(x):
    return 1.0 / (1.0 + jnp.exp(-x))


def _silu(x):
    return x * _sigmoid(x)


def _params(semantics, vmem_limit):
    return pltpu.CompilerParams(dimension_semantics=semantics, vmem_limit_bytes=vmem_limit)


def _rmsnorm_kernel(x_ref, g_ref, o_ref):
    x = x_ref[...]
    ms = jnp.mean(x * x, axis=-1, keepdims=True)
    o_ref[...] = ((x * lax.rsqrt(ms + EPS)) * g_ref[...]).astype(o_ref.dtype)


def _rmsnorm(x2d, g, out_dtype):
    rows, d = x2d.shape
    return pl.pallas_call(
        _rmsnorm_kernel,
        grid=(rows // NORM_ROWS,),
        in_specs=[pl.BlockSpec((NORM_ROWS, d), lambda i: (i, 0)),
                  pl.BlockSpec((1, d), lambda i: (0, 0))],
        out_specs=pl.BlockSpec((NORM_ROWS, d), lambda i: (i, 0)),
        out_shape=jax.ShapeDtypeStruct((rows, d), out_dtype),
        compiler_params=_params(("parallel",), MIX_VMEM_LIMIT),
        name="rmsnorm",
    )(x2d, g.reshape(1, d))


def _inproj_kernel(layer, n_col_blocks, n_row_tiles, a_ref, w_hbm, o_ref, wb_ref, stage_ref, sem):
    j = pl.program_id(0)
    i = pl.program_id(1)
    n_slots, kc, _ = stage_ref.shape

    def chunk_copy(col_block, c, slot):
        rows = pl.ds(pl.multiple_of(c * kc, kc), kc)
        cols = pl.ds(pl.multiple_of(col_block * IN_BN, IN_BN), IN_BN)
        return pltpu.make_async_copy(w_hbm.at[layer, rows, cols], stage_ref.at[slot], sem.at[slot])

    @pl.when((j == 0) & (i == 0))
    def _():
        copies = [chunk_copy(0, c, c % n_slots) for c in range(n_row_tiles)]
        for c in range(min(n_slots, n_row_tiles)):
            copies[c].start()
        for c in range(n_row_tiles):
            copies[c].wait()
            wb_ref[0, c * kc:(c + 1) * kc, :] = stage_ref[c % n_slots].astype(BF16)
            if c + n_slots < n_row_tiles:
                copies[c + n_slots].start()

        if n_col_blocks > 1:
            chunk_copy(1, 0, 0).start()

    o_ref[...] = jnp.dot(a_ref[...], wb_ref[j % 2], preferred_element_type=F32).astype(o_ref.dtype)

    @pl.when(j + 1 < n_col_blocks)
    def _():
        chunk_copy(j + 1, i, 0).wait()
        rows = pl.ds(pl.multiple_of(i * kc, kc), kc)
        wb_ref[(j + 1) % 2, rows, :] = stage_ref[0].astype(BF16)

        @pl.when(i + 1 < n_row_tiles)
        def _():
            chunk_copy(j + 1, i + 1, 0).start()

        @pl.when((i + 1 == n_row_tiles) & (j + 2 < n_col_blocks))
        def _():
            chunk_copy(j + 2, 0, 0).start()


def _inproj(a, w_stack, layer, out_dtype):
    m, k = a.shape
    n = w_stack.shape[2]
    bm = min(IN_BM, m // IN_MIN_ROW_TILES)
    n_row_tiles = m // bm
    kc = k // n_row_tiles
    n_slots = max(2, min(4, IN_STAGE_BYTES // (kc * IN_BN * 4)))
    kernel = functools.partial(_inproj_kernel, layer, n // IN_BN, n_row_tiles)
    return pl.pallas_call(
        kernel,
        grid=(n // IN_BN, n_row_tiles),
        in_specs=[pl.BlockSpec((bm, k), lambda j, i: (i, 0)),
                  pl.BlockSpec(memory_space=pl.ANY)],
        out_specs=pl.BlockSpec((bm, IN_BN), lambda j, i: (i, j)),
        out_shape=jax.ShapeDtypeStruct((m, n), out_dtype),
        scratch_shapes=[pltpu.VMEM((2, k, IN_BN), BF16),
                        pltpu.VMEM((n_slots, kc, IN_BN), F32),
                        pltpu.SemaphoreType.DMA((n_slots,))],
        compiler_params=_params(("arbitrary", "arbitrary"), MM_VMEM_LIMIT),
        name="inproj",
    )(a, w_stack)


def _rope_kernel(pos_ref, freq_ref, cos_ref, sin_ref):
    ang = pos_ref[...] * freq_ref[...]
    cos_ref[...] = jnp.cos(ang)
    sin_ref[...] = jnp.sin(ang)


def _rope_tables(pos_col, inv_freq):
    rows = pos_col.shape[0]
    half = inv_freq.shape[1]
    spec = pl.BlockSpec((MIX_ROWS, half), lambda i: (i, 0))
    return pl.pallas_call(
        _rope_kernel,
        grid=(rows // MIX_ROWS,),
        in_specs=[pl.BlockSpec((MIX_ROWS, 1), lambda i: (i, 0)),
                  pl.BlockSpec((1, half), lambda i: (0, 0))],
        out_specs=[spec, spec],
        out_shape=[jax.ShapeDtypeStruct((rows, half), F32)] * 2,
        compiler_params=_params(("parallel",), MIX_VMEM_LIMIT),
        name="rope_tables",
    )(pos_col, inv_freq)


def _slice_spec(rows, col_block):
    return pl.BlockSpec((rows, GROUP_W), lambda i: (i, col_block))


def _row_spec(n):
    return pl.BlockSpec((1, n), lambda i: (0, 0))


def _sgu_kernel(u_ref, v_ref, g_ref, ng_ref, nb_ref, w_ref, bias_ref, o_ref, vn_ref):
    v = v_ref[...].astype(F32)
    mu = jnp.mean(v, axis=-1, keepdims=True)
    var = jnp.mean(jnp.square(v - mu), axis=-1, keepdims=True)
    vn = ((v - mu) * lax.rsqrt(var + EPS)) * ng_ref[...] + nb_ref[...]
    vn_ref[...] = vn.astype(BF16)

    t = lax.broadcasted_iota(jnp.int32, (SGU_CHUNK, SGU_CHUNK), 0)
    s = lax.broadcasted_iota(jnp.int32, (SGU_CHUNK, SGU_CHUNK), 1)
    causal = t >= s
    for h in range(SGU_HEADS):
        w = jnp.where(causal, w_ref[h], 0.0).astype(BF16)
        cols = slice(h * SGU_CHUNK, (h + 1) * SGU_CHUNK)
        for n in range(MIX_ROWS // SGU_CHUNK):
            rows = slice(n * SGU_CHUNK, (n + 1) * SGU_CHUNK)
            mix = jnp.dot(w, vn_ref[rows, cols], preferred_element_type=F32) + bias_ref[:, cols]
            u = u_ref[rows, cols].astype(F32)
            gate = g_ref[rows, cols].astype(F32)
            o_ref[rows, cols] = ((u * mix) * _silu(gate)).astype(o_ref.dtype)


def _sgu(proj, ng, nb, w, bias_full):
    rows = proj.shape[0]
    return pl.pallas_call(
        _sgu_kernel,
        grid=(rows // MIX_ROWS,),
        in_specs=[_slice_spec(MIX_ROWS, 0), _slice_spec(MIX_ROWS, 1), _slice_spec(MIX_ROWS, 2),
                  _row_spec(GROUP_W), _row_spec(GROUP_W),
                  pl.BlockSpec((SGU_HEADS, SGU_CHUNK, SGU_CHUNK), lambda i: (0, 0, 0)),
                  pl.BlockSpec((SGU_CHUNK, GROUP_W), lambda i: (0, 0))],
        out_specs=pl.BlockSpec((MIX_ROWS, GROUP_W), lambda i: (i, 0)),
        out_shape=jax.ShapeDtypeStruct((rows, GROUP_W), BF16),
        scratch_shapes=[pltpu.VMEM((MIX_ROWS, GROUP_W), BF16)],
        compiler_params=_params(("parallel",), MIX_VMEM_LIMIT),
        name="sgu",
    )(proj, proj, proj, ng.reshape(1, -1), nb.reshape(1, -1), w, bias_full)


def _retention_kernel(steps_per_seq, logg_ref, q_ref, k_ref, v_ref, g_ref, cos_ref, sin_ref,
                      ng_ref, nb_ref, o_ref, state_ref, decay_ref):
    i = pl.program_id(0)

    @pl.when((i % steps_per_seq) == 0)
    def _():
        state_ref[...] = jnp.zeros_like(state_ref)

    c = RET_CHUNK
    half = RET_HEAD_DIM // 2
    scale = RET_HEAD_DIM ** -0.5
    t_row = lax.broadcasted_iota(jnp.int32, (c, c), 0)
    s_col = lax.broadcasted_iota(jnp.int32, (c, c), 1)
    diff = (t_row - s_col).astype(F32)
    t_vec = lax.broadcasted_iota(jnp.int32, (c, 1), 0).astype(F32)
    for h in range(RET_HEADS):
        log_g = logg_ref[h]
        decay_ref[h] = jnp.where(diff >= 0, jnp.exp(log_g * jnp.maximum(diff, 0.0)), 0.0)

    def chunk(ci, carry):
        rows = pl.ds(pl.multiple_of(ci * c, c), c)
        cos = cos_ref[rows, :]
        sin = sin_ref[rows, :]

        def rot(ref, lo):
            x1 = ref[rows, lo:lo + half].astype(F32)
            x2 = ref[rows, lo + half:lo + 2 * half].astype(F32)
            return x1 * cos - x2 * sin, x2 * cos + x1 * sin

        for h in range(RET_HEADS):
            lo = h * RET_HEAD_DIM
            cols = slice(lo, lo + RET_HEAD_DIM)
            log_g = logg_ref[h]
            q1, q2 = rot(q_ref, lo)
            k1, k2 = rot(k_ref, lo)
            q = jnp.concatenate([q1, q2], axis=-1)
            k = jnp.concatenate([k1, k2], axis=-1) * scale
            v = v_ref[rows, cols].astype(BF16)

            k_decay = jnp.exp(log_g * ((c - 1.0) - t_vec))
            q_decay = jnp.exp(log_g * (t_vec + 1.0))
            chunk_decay = jnp.exp(log_g * float(c))

            scores = lax.dot_general(q.astype(BF16), k.astype(BF16), (((1,), (1,)), ((), ())),
                                     preferred_element_type=F32) * decay_ref[h]
            intra = jnp.dot(scores.astype(BF16), v, preferred_element_type=F32)
            state = state_ref[h]
            cross = jnp.dot((q * q_decay).astype(BF16), state.astype(BF16),
                            preferred_element_type=F32)
            kv = lax.dot_general((k * k_decay).astype(BF16), v, (((0,), (0,)), ((), ())),
                                 preferred_element_type=F32)
            state_ref[h] = chunk_decay * state + kv

            r = intra + cross
            mu = jnp.mean(r, axis=-1, keepdims=True)
            var = jnp.mean(jnp.square(r - mu), axis=-1, keepdims=True)
            y = ((r - mu) * lax.rsqrt(var + EPS)) * ng_ref[:, cols] + nb_ref[:, cols]
            gate = g_ref[rows, cols].astype(F32)
            o_ref[rows, cols] = (y * _silu(gate)).astype(o_ref.dtype)
        return carry

    lax.fori_loop(0, MIX_ROWS // c, chunk, 0)


def _retention(proj, seq_len, log_g, cos, sin, ng, nb):
    rows = proj.shape[0]
    half = RET_HEAD_DIM // 2
    kernel = functools.partial(_retention_kernel, seq_len // MIX_ROWS)
    tab_spec = pl.BlockSpec((MIX_ROWS, half), lambda i: (i, 0))
    return pl.pallas_call(
        kernel,
        grid=(rows // MIX_ROWS,),
        in_specs=[pl.BlockSpec(memory_space=pltpu.SMEM),
                  _slice_spec(MIX_ROWS, 6), _slice_spec(MIX_ROWS, 7), _slice_spec(MIX_ROWS, 8),
                  _slice_spec(MIX_ROWS, 9), tab_spec, tab_spec,
                  _row_spec(GROUP_W), _row_spec(GROUP_W)],
        out_specs=pl.BlockSpec((MIX_ROWS, GROUP_W), lambda i: (i, 0)),
        out_shape=jax.ShapeDtypeStruct((rows, GROUP_W), BF16),
        scratch_shapes=[pltpu.VMEM((RET_HEADS, RET_HEAD_DIM, RET_HEAD_DIM), F32),
                        pltpu.VMEM((RET_HEADS, RET_CHUNK, RET_CHUNK), F32)],
        compiler_params=_params(("arbitrary",), MIX_VMEM_LIMIT),
        name="retention",
    )(log_g, proj, proj, proj, proj, cos, sin, ng.reshape(1, -1), nb.reshape(1, -1))


def _memattn_kernel(q_ref, g_ref, k_ref, v_ref, o_ref):
    scale = MEM_HEAD_DIM ** -0.5
    for h in range(MEM_HEADS):
        cols = slice(h * MEM_HEAD_DIM, (h + 1) * MEM_HEAD_DIM)
        s = lax.dot_general(q_ref[:, cols], k_ref[:, cols], (((1,), (1,)), ((), ())),
                            preferred_element_type=F32) * scale
        m = jnp.max(s, axis=-1, keepdims=True)
        p = jnp.exp(s - m)
        denom = jnp.sum(p, axis=-1, keepdims=True)
        p = p / denom
        o = jnp.dot(p.astype(BF16), v_ref[:, cols], preferred_element_type=F32)
        gate = g_ref[:, cols].astype(F32)
        o_ref[:, cols] = (o * _silu(gate)).astype(o_ref.dtype)


def _memattn(proj, kv, seq_len):
    rows = proj.shape[0]
    steps_per_seq = seq_len // MIX_ROWS
    return pl.pallas_call(
        _memattn_kernel,
        grid=(rows // MIX_ROWS,),
        in_specs=[_slice_spec(MIX_ROWS, 10), _slice_spec(MIX_ROWS, 11),
                  pl.BlockSpec((MEM_LEN, GROUP_W), lambda i: (i // steps_per_seq, 0)),
                  pl.BlockSpec((MEM_LEN, GROUP_W), lambda i: (i // steps_per_seq, 1))],
        out_specs=pl.BlockSpec((MIX_ROWS, GROUP_W), lambda i: (i, 0)),
        out_shape=jax.ShapeDtypeStruct((rows, GROUP_W), BF16),
        compiler_params=_params(("parallel",), MIX_VMEM_LIMIT),
        name="memattn",
    )(proj, proj, kv, kv)


def _rows2(start, n):
    return pl.ds(2 * start, n, stride=2)


def _derived_zero(x):
    bits = lax.bitcast_convert_type(x, jnp.uint32)
    bits = lax.shift_right_logical(lax.shift_right_logical(bits, jnp.uint32(16)), jnp.uint32(16))
    return lax.bitcast_convert_type(bits, F32)


def _conv_rows(seq_start, row0, unrolled, ca_ref, cb_ref, cg_ref, cw_ref, cbias_ref, cng_ref,
               cnb_ref, dst_ref, buf_ref):
    n_slabs = GROUP_W // LANES
    for s in range(n_slabs):
        cols = slice(s * LANES, (s + 1) * LANES)
        tail = buf_ref[s, _rows2(CONV_QR, CONV_HALO), :]
        buf_ref[s, _rows2(0, CONV_HALO), :] = jnp.where(seq_start, 0.0, tail)
        buf_ref[s, _rows2(CONV_HALO, CONV_QR), :] = (
            ca_ref[:, cols].astype(F32) * _sigmoid(cb_ref[:, cols].astype(F32)))

    prev_acc = [None]

    def chunk(c):
        r0 = c * CONV_RC
        for s in range(n_slabs):
            cols = slice(s * LANES, (s + 1) * LANES)
            acc = jnp.broadcast_to(cbias_ref[:, cols], (CONV_RC, LANES))
            if unrolled and prev_acc[0] is not None:
                acc = acc + _derived_zero(prev_acc[0])
            for k in range(CONV_WIDTH):
                start = r0 + (CONV_HALO - (CONV_WIDTH - 1) + k)
                acc = acc + cw_ref[k:k + 1, cols] * buf_ref[s, _rows2(start, CONV_RC), :]
            prev_acc[0] = acc
            mu = jnp.mean(acc, axis=-1, keepdims=True)
            var = jnp.mean(jnp.square(acc - mu), axis=-1, keepdims=True)
            y = ((acc - mu) * lax.rsqrt(var + EPS)) * cng_ref[:, cols] + cnb_ref[:, cols]
            if unrolled:
                gate = cg_ref[r0:r0 + CONV_RC, cols].astype(F32)
            else:
                gate = cg_ref[pl.ds(pl.multiple_of(r0, CONV_RC), CONV_RC), cols].astype(F32)
            out_rows = pl.ds(pl.multiple_of(row0 + r0, CONV_RC), CONV_RC)
            dst_ref[out_rows, cols] = (_silu(y) * _silu(gate)).astype(dst_ref.dtype)

    if unrolled:
        for c in range(CONV_QR // CONV_RC):
            chunk(c)
    else:
        def body(c, carry):
            chunk(c)
            return carry
        lax.fori_loop(0, CONV_QR // CONV_RC, body, 0)


def _outproj_kernel(emit_x, steps_per_seq, ya_ref, yc_ref, ym_ref, w_ref, x_ref, ng_ref,
                    ca_ref, cb_ref, cg_ref, cw_ref, cbias_ref, cng_ref, cnb_ref, *rest):
    if emit_x:
        xo_ref, no_ref, yb_ref, buf_ref, ssq_ref, xrow_ref = rest
    else:
        xo_ref = None
        no_ref, yb_ref, buf_ref, ssq_ref = rest
        xrow_ref = no_ref
    i = pl.program_id(0)
    j = pl.program_id(1)
    step = i * OUT_NJ + j
    seq_start = (step % steps_per_seq) == 0
    conv_args = (ca_ref, cb_ref, cg_ref, cw_ref, cbias_ref, cng_ref, cnb_ref)
    row0 = j * CONV_QR

    @pl.when(i == 0)
    def _():
        @pl.when(j == 0)
        def _():
            for s in range(GROUP_W // LANES):
                buf_ref[s, _rows2(CONV_QR, CONV_HALO), :] = jnp.zeros((CONV_HALO, LANES), F32)

        _conv_rows(seq_start, row0, False, *conv_args, yb_ref.at[0], buf_ref)

    @pl.when(i > 0)
    def _():
        acc = x_ref[...]
        ys = ((1, yb_ref[(i - 1) % 2]), (0, ya_ref[...]), (2, yc_ref[...]), (3, ym_ref[...]))
        for g, y in ys:
            acc = acc + jnp.dot(y, w_ref[g * GROUP_W:(g + 1) * GROUP_W, :],
                                preferred_element_type=F32)
        if emit_x:
            xo_ref[...] = acc
        col0 = pl.multiple_of(j * OUT_BN, OUT_BN)
        xrow_ref[:, pl.ds(col0, OUT_BN)] = acc
        ssq_ref[j] = jnp.sum(acc * acc, axis=-1, keepdims=True)

        _conv_rows(seq_start, row0, True, *conv_args, yb_ref.at[i % 2], buf_ref)

        @pl.when(j == OUT_NJ - 1)
        def _():
            ssq = ssq_ref[0]
            for jj in range(1, OUT_NJ):
                ssq = ssq + ssq_ref[jj]
            rstd = lax.rsqrt(ssq * (1.0 / D_MODEL) + EPS)
            for jj in range(OUT_NJ):
                cols = slice(jj * OUT_BN, (jj + 1) * OUT_BN)
                no_ref[:, cols] = ((xrow_ref[:, cols] * rstd) * ng_ref[:, cols]).astype(no_ref.dtype)


def _outproj(ys, proj, seq_len, w_stack, layer, x2d, next_g, conv_w, conv_b, conv_ng, conv_nb,
             emit_x, norm_dtype):
    m, d = x2d.shape
    k = w_stack.shape[1]
    nb = m // OUT_BM
    steps_per_seq = seq_len // CONV_QR
    last_q = m // CONV_QR - 1

    def prev(i):
        return jnp.maximum(i - 1, 0)

    y_spec = pl.BlockSpec((OUT_BM, GROUP_W), lambda i, j: (prev(i), 0))

    def conv_spec(col):
        return pl.BlockSpec((CONV_QR, GROUP_W),
                            lambda i, j: (jnp.minimum(i * OUT_NJ + j, last_q), col))

    def tile_idx(i, j):
        return (prev(i), jnp.where(i == 0, 0, j))

    row_full = pl.BlockSpec((1, d), lambda i, j: (0, 0))
    row_grp = pl.BlockSpec((1, GROUP_W), lambda i, j: (0, 0))
    in_specs = [y_spec, y_spec, y_spec,
                pl.BlockSpec((None, k, OUT_BN), lambda i, j: (layer, 0, j)),
                pl.BlockSpec((OUT_BM, OUT_BN), tile_idx),
                row_full,
                conv_spec(3), conv_spec(4), conv_spec(5),
                pl.BlockSpec((CONV_WIDTH, GROUP_W), lambda i, j: (0, 0)),
                row_grp, row_grp, row_grp]
    norm_spec = pl.BlockSpec((OUT_BM, d), lambda i, j: (prev(i), 0))
    norm_shape = jax.ShapeDtypeStruct((m, d), norm_dtype)
    if emit_x:
        out_specs = [pl.BlockSpec((OUT_BM, OUT_BN), tile_idx), norm_spec]
        out_shape = [jax.ShapeDtypeStruct((m, d), F32), norm_shape]
    else:
        out_specs = [norm_spec]
        out_shape = [norm_shape]
    scratch = [pltpu.VMEM((2, OUT_BM, GROUP_W), BF16),
               pltpu.VMEM((GROUP_W // LANES, 2 * (CONV_HALO + CONV_QR), LANES), F32),
               pltpu.VMEM((OUT_NJ, OUT_BM, 1), F32)]
    if emit_x:
        scratch.append(pltpu.VMEM((OUT_BM, d), F32))
    else:
        assert norm_dtype == F32
    ya, yc, ym = ys
    outs = pl.pallas_call(
        functools.partial(_outproj_kernel, emit_x, steps_per_seq),
        grid=(nb + 1, OUT_NJ),
        in_specs=in_specs,
        out_specs=out_specs,
        out_shape=out_shape,
        scratch_shapes=scratch,
        compiler_params=_params(("arbitrary", "arbitrary"), MM_VMEM_LIMIT),
        name="outproj",
    )(ya, yc, ym, w_stack, x2d, next_g.reshape(1, d), proj, proj, proj,
      conv_w, conv_b.reshape(1, -1), conv_ng.reshape(1, -1), conv_nb.reshape(1, -1))
    return outs if emit_x else (None, outs[0])


def kernel(x, mem, positions, norm_g, w_in, sgu_norm_g, sgu_norm_b, sgu_w, sgu_b,
           conv_w, conv_b, conv_norm_g, conv_norm_b, ret_norm_g, ret_norm_b,
           mem_norm_g, w_mem_kv, w_out, final_norm_g):
    bsz, s_len, d = x.shape
    depth = w_in.shape[0]
    assert d == D_MODEL and w_in.shape[2] == N_IN_SLICES * GROUP_W
    assert s_len % OUT_BM == 0 and s_len % MIX_ROWS == 0
    assert MIX_ROWS % SGU_CHUNK == 0 and MIX_ROWS % RET_CHUNK == 0
    assert GROUP_W // CONV_GROUPS == LANES and CONV_HALO >= CONV_WIDTH - 1
    assert CONV_QR % CONV_RC == 0 and CONV_QR >= CONV_HALO
    rows = bsz * s_len
    x2d = x.reshape(rows, d)
    mem2d = mem.reshape(bsz * MEM_LEN, d)

    half = RET_HEAD_DIM // 2
    inv_freq = jnp.power(ROPE_BASE, -jnp.arange(half, dtype=F32) / half).reshape(1, half)
    pos_col = positions.astype(F32).reshape(rows, 1)
    cos, sin = _rope_tables(pos_col, inv_freq)
    log_g = jnp.log(1.0 - jnp.power(2.0, -5.0 - jnp.arange(RET_HEADS, dtype=F32)))
    w_out_bf = w_out.astype(BF16)

    h = _rmsnorm(x2d, norm_g[0], BF16)
    out = None
    for l in range(depth):
        last = l == depth - 1
        proj = _inproj(h, w_in, l, BF16)

        bias_full = jnp.repeat(sgu_b[l].T, GROUP_W // SGU_HEADS, axis=1)
        y_a = _sgu(proj, sgu_norm_g[l], sgu_norm_b[l], sgu_w[l], bias_full)
        y_c = _retention(proj, s_len, log_g, cos, sin, ret_norm_g[l], ret_norm_b[l])

        mem_n = _rmsnorm(mem2d, mem_norm_g[l], BF16)
        kv = _inproj(mem_n, w_mem_kv, l, BF16)
        y_m = _memattn(proj, kv, s_len)

        next_g = final_norm_g if last else norm_g[l + 1]
        x2d, normed = _outproj((y_a, y_c, y_m), proj, s_len, w_out_bf, l, x2d, next_g,
                               conv_w[l], conv_b[l], conv_norm_g[l], conv_norm_b[l],
                               emit_x=not last, norm_dtype=F32 if last else BF16)
        if last:
            out = normed
        else:
            h = normed
    return out.reshape(bsz, s_len, d)
```

```python
import functools

import jax
import jax.numpy as jnp
from jax import lax
from jax.experimental import pallas as pl
from jax.experimental.pallas import tpu as pltpu

F32 = jnp.float32
BF16 = jnp.bfloat16

D_MODEL = 4096
GROUP_W = 1024
N_IN_SLICES = 12
SGU_CHUNK = 128
SGU_HEADS = 8
CONV_WIDTH = 31
CONV_GROUPS = 8
RET_HEADS = 4
RET_HEAD_DIM = 256
MEM_LEN = 256
MEM_HEADS = 4
MEM_HEAD_DIM = 256
ROPE_BASE = 10000.0
EPS = 1e-6

V7X_VMEM_BYTES = 64 * 1024 * 1024
LANES = 128

NORM_ROWS = 256
IN_BM = 1024
IN_BN = 1024
IN_MIN_ROW_TILES = 4
IN_STAGE_BYTES = 8 * 1024 * 1024
OUT_BM = 512
OUT_BN = 1024
OUT_NJ = D_MODEL // OUT_BN
CONV_QR = OUT_BM // OUT_NJ
CONV_RC = 16
CONV_HALO = 32
MIX_ROWS = 1024
RET_CHUNK = 256
MM_VMEM_LIMIT = 58 * 1024 * 1024
MIX_VMEM_LIMIT = 48 * 1024 * 1024


def _sigmoid(x):
    return 1.0 / (1.0 + jnp.exp(-x))


def _silu(x):
    return x * _sigmoid(x)


def _params(semantics, vmem_limit):
    return pltpu.CompilerParams(dimension_semantics=semantics, vmem_limit_bytes=vmem_limit)


def _rmsnorm_kernel(x_ref, g_ref, o_ref):
    x = x_ref[...]
    ms = jnp.mean(x * x, axis=-1, keepdims=True)
    o_ref[...] = ((x * lax.rsqrt(ms + EPS)) * g_ref[...]).astype(o_ref.dtype)


def _rmsnorm(x2d, g, out_dtype):
    rows, d = x2d.shape
    return pl.pallas_call(
        _rmsnorm_kernel,
        grid=(rows // NORM_ROWS,),
        in_specs=[pl.BlockSpec((NORM_ROWS, d), lambda i: (i, 0)),
                  pl.BlockSpec((1, d), lambda i: (0, 0))],
        out_specs=pl.BlockSpec((NORM_ROWS, d), lambda i: (i, 0)),
        out_shape=jax.ShapeDtypeStruct((rows, d), out_dtype),
        compiler_params=_params(("parallel",), MIX_VMEM_LIMIT),
        name="rmsnorm",
    )(x2d, g.reshape(1, d))


def _inproj_kernel(layer, n_col_blocks, n_row_tiles, with_side_cast, a_ref, w_hbm, *rest):
    if with_side_cast:
        side_ref, o_ref, side_out_ref, wb_ref, stage_ref, sem = rest
        side_out_ref[...] = side_ref[...].astype(BF16)
    else:
        o_ref, wb_ref, stage_ref, sem = rest
    j = pl.program_id(0)
    i = pl.program_id(1)
    n_slots, kc, _ = stage_ref.shape

    def chunk_copy(col_block, c, slot):
        rows = pl.ds(pl.multiple_of(c * kc, kc), kc)
        cols = pl.ds(pl.multiple_of(col_block * IN_BN, IN_BN), IN_BN)
        return pltpu.make_async_copy(w_hbm.at[layer, rows, cols], stage_ref.at[slot], sem.at[slot])

    @pl.when((j == 0) & (i == 0))
    def _():
        copies = [chunk_copy(0, c, c % n_slots) for c in range(n_row_tiles)]
        for c in range(min(n_slots, n_row_tiles)):
            copies[c].start()
        for c in range(n_row_tiles):
            copies[c].wait()
            wb_ref[0, c * kc:(c + 1) * kc, :] = stage_ref[c % n_slots].astype(BF16)
            if c + n_slots < n_row_tiles:
                copies[c + n_slots].start()

        if n_col_blocks > 1:
            chunk_copy(1, 0, 0).start()

    o_ref[...] = jnp.dot(a_ref[...], wb_ref[j % 2], preferred_element_type=F32).astype(o_ref.dtype)

    @pl.when(j + 1 < n_col_blocks)
    def _():
        chunk_copy(j + 1, i, 0).wait()
        rows = pl.ds(pl.multiple_of(i * kc, kc), kc)
        wb_ref[(j + 1) % 2, rows, :] = stage_ref[0].astype(BF16)

        @pl.when(i + 1 < n_row_tiles)
        def _():
            chunk_copy(j + 1, i + 1, 0).start()

        @pl.when((i + 1 == n_row_tiles) & (j + 2 < n_col_blocks))
        def _():
            chunk_copy(j + 2, 0, 0).start()


def _inproj(a, w_stack, layer, out_dtype, side_stack=None):
    m, k = a.shape
    n = w_stack.shape[2]
    bm = min(IN_BM, m // IN_MIN_ROW_TILES)
    n_row_tiles = m // bm
    n_col_blocks = n // IN_BN
    kc = k // n_row_tiles
    n_slots = max(2, min(4, IN_STAGE_BYTES // (kc * IN_BN * 4)))
    with_side = side_stack is not None
    kernel = functools.partial(_inproj_kernel, layer, n_col_blocks, n_row_tiles, with_side)
    in_specs = [pl.BlockSpec((bm, k), lambda j, i: (i, 0)), pl.BlockSpec(memory_space=pl.ANY)]
    out_specs = [pl.BlockSpec((bm, IN_BN), lambda j, i: (i, j))]
    out_shape = [jax.ShapeDtypeStruct((m, n), out_dtype)]
    args = [a, w_stack]
    if with_side:
        _, sr, sc = side_stack.shape
        side_cols = sc // IN_BN
        assert sr % n_row_tiles == 0 and sc % IN_BN == 0 and side_cols <= n_col_blocks

        def side_idx(j, i):
            return (jnp.where(j < side_cols, i, n_row_tiles - 1), jnp.minimum(j, side_cols - 1))

        in_specs.append(pl.BlockSpec((None, sr // n_row_tiles, IN_BN),
                                     lambda j, i: (layer,) + side_idx(j, i)))
        out_specs.append(pl.BlockSpec((sr // n_row_tiles, IN_BN), side_idx))
        out_shape.append(jax.ShapeDtypeStruct((sr, sc), BF16))
        args.append(side_stack)
    outs = pl.pallas_call(
        kernel,
        grid=(n_col_blocks, n_row_tiles),
        in_specs=in_specs,
        out_specs=out_specs,
        out_shape=out_shape,
        scratch_shapes=[pltpu.VMEM((2, k, IN_BN), BF16),
                        pltpu.VMEM((n_slots, kc, IN_BN), F32),
                        pltpu.SemaphoreType.DMA((n_slots,))],
        compiler_params=_params(("arbitrary", "arbitrary"), MM_VMEM_LIMIT),
        name="inproj",
    )(*args)
    return outs if with_side else outs[0]


def _rope_kernel(pos_ref, freq_ref, cos_ref, sin_ref):
    ang = pos_ref[...] * freq_ref[...]
    cos_ref[...] = jnp.cos(ang)
    sin_ref[...] = jnp.sin(ang)


def _rope_tables(pos_col, inv_freq):
    rows = pos_col.shape[0]
    half = inv_freq.shape[1]
    spec = pl.BlockSpec((MIX_ROWS, half), lambda i: (i, 0))
    return pl.pallas_call(
        _rope_kernel,
        grid=(rows // MIX_ROWS,),
        in_specs=[pl.BlockSpec((MIX_ROWS, 1), lambda i: (i, 0)),
                  pl.BlockSpec((1, half), lambda i: (0, 0))],
        out_specs=[spec, spec],
        out_shape=[jax.ShapeDtypeStruct((rows, half), F32)] * 2,
        compiler_params=_params(("parallel",), MIX_VMEM_LIMIT),
        name="rope_tables",
    )(pos_col, inv_freq)


def _slice_spec(rows, col_block):
    return pl.BlockSpec((rows, GROUP_W), lambda i: (i, col_block))


def _row_spec(n):
    return pl.BlockSpec((1, n), lambda i: (0, 0))


def _sgu_kernel(u_ref, v_ref, g_ref, ng_ref, nb_ref, w_ref, bias_ref, o_ref, vn_ref):
    v = v_ref[...].astype(F32)
    mu = jnp.mean(v, axis=-1, keepdims=True)
    var = jnp.mean(jnp.square(v - mu), axis=-1, keepdims=True)
    vn = ((v - mu) * lax.rsqrt(var + EPS)) * ng_ref[...] + nb_ref[...]
    vn_ref[...] = vn.astype(BF16)

    t = lax.broadcasted_iota(jnp.int32, (SGU_CHUNK, SGU_CHUNK), 0)
    s = lax.broadcasted_iota(jnp.int32, (SGU_CHUNK, SGU_CHUNK), 1)
    causal = t >= s
    for h in range(SGU_HEADS):
        w = jnp.where(causal, w_ref[h], 0.0).astype(BF16)
        cols = slice(h * SGU_CHUNK, (h + 1) * SGU_CHUNK)
        for n in range(MIX_ROWS // SGU_CHUNK):
            rows = slice(n * SGU_CHUNK, (n + 1) * SGU_CHUNK)
            mix = jnp.dot(w, vn_ref[rows, cols], preferred_element_type=F32) + bias_ref[:, cols]
            u = u_ref[rows, cols].astype(F32)
            gate = g_ref[rows, cols].astype(F32)
            o_ref[rows, cols] = ((u * mix) * _silu(gate)).astype(o_ref.dtype)


def _sgu(proj, ng, nb, w, bias_full):
    rows = proj.shape[0]
    return pl.pallas_call(
        _sgu_kernel,
        grid=(rows // MIX_ROWS,),
        in_specs=[_slice_spec(MIX_ROWS, 0), _slice_spec(MIX_ROWS, 1), _slice_spec(MIX_ROWS, 2),
                  _row_spec(GROUP_W), _row_spec(GROUP_W),
                  pl.BlockSpec((SGU_HEADS, SGU_CHUNK, SGU_CHUNK), lambda i: (0, 0, 0)),
                  pl.BlockSpec((SGU_CHUNK, GROUP_W), lambda i: (0, 0))],
        out_specs=pl.BlockSpec((MIX_ROWS, GROUP_W), lambda i: (i, 0)),
        out_shape=jax.ShapeDtypeStruct((rows, GROUP_W), BF16),
        scratch_shapes=[pltpu.VMEM((MIX_ROWS, GROUP_W), BF16)],
        compiler_params=_params(("parallel",), MIX_VMEM_LIMIT),
        name="sgu",
    )(proj, proj, proj, ng.reshape(1, -1), nb.reshape(1, -1), w, bias_full)


def _retention_kernel(steps_per_seq, logg_ref, q_ref, k_ref, v_ref, g_ref, cos_ref, sin_ref,
                      ng_ref, nb_ref, o_ref, state_ref, decay_ref):
    i = pl.program_id(0)

    @pl.when((i % steps_per_seq) == 0)
    def _():
        state_ref[...] = jnp.zeros_like(state_ref)

    c = RET_CHUNK
    half = RET_HEAD_DIM // 2
    scale = RET_HEAD_DIM ** -0.5
    t_row = lax.broadcasted_iota(jnp.int32, (c, c), 0)
    s_col = lax.broadcasted_iota(jnp.int32, (c, c), 1)
    diff = (t_row - s_col).astype(F32)
    t_vec = lax.broadcasted_iota(jnp.int32, (c, 1), 0).astype(F32)
    for h in range(RET_HEADS):
        log_g = logg_ref[h]
        decay_ref[h] = jnp.where(diff >= 0, jnp.exp(log_g * jnp.maximum(diff, 0.0)), 0.0)

    def chunk(ci, carry):
        rows = pl.ds(pl.multiple_of(ci * c, c), c)
        cos = cos_ref[rows, :]
        sin = sin_ref[rows, :]

        def rot(ref, lo):
            x1 = ref[rows, lo:lo + half].astype(F32)
            x2 = ref[rows, lo + half:lo + 2 * half].astype(F32)
            return x1 * cos - x2 * sin, x2 * cos + x1 * sin

        for h in range(RET_HEADS):
            lo = h * RET_HEAD_DIM
            cols = slice(lo, lo + RET_HEAD_DIM)
            log_g = logg_ref[h]
            q1, q2 = rot(q_ref, lo)
            k1, k2 = rot(k_ref, lo)
            q = jnp.concatenate([q1, q2], axis=-1)
            k = jnp.concatenate([k1, k2], axis=-1) * scale
            v = v_ref[rows, cols].astype(BF16)

            k_decay = jnp.exp(log_g * ((c - 1.0) - t_vec))
            q_decay = jnp.exp(log_g * (t_vec + 1.0))
            chunk_decay = jnp.exp(log_g * float(c))

            scores = lax.dot_general(q.astype(BF16), k.astype(BF16), (((1,), (1,)), ((), ())),
                                     preferred_element_type=F32) * decay_ref[h]
            intra = jnp.dot(scores.astype(BF16), v, preferred_element_type=F32)
            state = state_ref[h]
            cross = jnp.dot((q * q_decay).astype(BF16), state.astype(BF16),
                            preferred_element_type=F32)
            kv = lax.dot_general((k * k_decay).astype(BF16), v, (((0,), (0,)), ((), ())),
                                 preferred_element_type=F32)
            state_ref[h] = chunk_decay * state + kv

            r = intra + cross
            mu = jnp.mean(r, axis=-1, keepdims=True)
            var = jnp.mean(jnp.square(r - mu), axis=-1, keepdims=True)
            y = ((r - mu) * lax.rsqrt(var + EPS)) * ng_ref[:, cols] + nb_ref[:, cols]
            gate = g_ref[rows, cols].astype(F32)
            o_ref[rows, cols] = (y * _silu(gate)).astype(o_ref.dtype)
        return carry

    lax.fori_loop(0, MIX_ROWS // c, chunk, 0)


def _retention(proj, seq_len, log_g, cos, sin, ng, nb):
    rows = proj.shape[0]
    half = RET_HEAD_DIM // 2
    kernel = functools.partial(_retention_kernel, seq_len // MIX_ROWS)
    tab_spec = pl.BlockSpec((MIX_ROWS, half), lambda i: (i, 0))
    return pl.pallas_call(
        kernel,
        grid=(rows // MIX_ROWS,),
        in_specs=[pl.BlockSpec(memory_space=pltpu.SMEM),
                  _slice_spec(MIX_ROWS, 6), _slice_spec(MIX_ROWS, 7), _slice_spec(MIX_ROWS, 8),
                  _slice_spec(MIX_ROWS, 9), tab_spec, tab_spec,
                  _row_spec(GROUP_W), _row_spec(GROUP_W)],
        out_specs=pl.BlockSpec((MIX_ROWS, GROUP_W), lambda i: (i, 0)),
        out_shape=jax.ShapeDtypeStruct((rows, GROUP_W), BF16),
        scratch_shapes=[pltpu.VMEM((RET_HEADS, RET_HEAD_DIM, RET_HEAD_DIM), F32),
                        pltpu.VMEM((RET_HEADS, RET_CHUNK, RET_CHUNK), F32)],
        compiler_params=_params(("arbitrary",), MIX_VMEM_LIMIT),
        name="retention",
    )(log_g, proj, proj, proj, proj, cos, sin, ng.reshape(1, -1), nb.reshape(1, -1))


def _memattn_kernel(q_ref, g_ref, k_ref, v_ref, o_ref):
    scale = MEM_HEAD_DIM ** -0.5
    for h in range(MEM_HEADS):
        cols = slice(h * MEM_HEAD_DIM, (h + 1) * MEM_HEAD_DIM)
        s = lax.dot_general(q_ref[:, cols], k_ref[:, cols], (((1,), (1,)), ((), ())),
                            preferred_element_type=F32) * scale
        m = jnp.max(s, axis=-1, keepdims=True)
        p = jnp.exp(s - m)
        denom = jnp.sum(p, axis=-1, keepdims=True)
        p = p / denom
        o = jnp.dot(p.astype(BF16), v_ref[:, cols], preferred_element_type=F32)
        gate = g_ref[:, cols].astype(F32)
        o_ref[:, cols] = (o * _silu(gate)).astype(o_ref.dtype)


def _memattn(proj, kv, seq_len):
    rows = proj.shape[0]
    steps_per_seq = seq_len // MIX_ROWS
    return pl.pallas_call(
        _memattn_kernel,
        grid=(rows // MIX_ROWS,),
        in_specs=[_slice_spec(MIX_ROWS, 10), _slice_spec(MIX_ROWS, 11),
                  pl.BlockSpec((MEM_LEN, GROUP_W), lambda i: (i // steps_per_seq, 0)),
                  pl.BlockSpec((MEM_LEN, GROUP_W), lambda i: (i // steps_per_seq, 1))],
        out_specs=pl.BlockSpec((MIX_ROWS, GROUP_W), lambda i: (i, 0)),
        out_shape=jax.ShapeDtypeStruct((rows, GROUP_W), BF16),
        compiler_params=_params(("parallel",), MIX_VMEM_LIMIT),
        name="memattn",
    )(proj, proj, kv, kv)


def _rows2(start, n):
    return pl.ds(2 * start, n, stride=2)


def _derived_zero(x):
    bits = lax.bitcast_convert_type(x, jnp.uint32)
    bits = lax.shift_right_logical(lax.shift_right_logical(bits, jnp.uint32(16)), jnp.uint32(16))
    return lax.bitcast_convert_type(bits, F32)


def _conv_rows(seq_start, row0, unrolled, ca_ref, cb_ref, cg_ref, cw_ref, cbias_ref, cng_ref,
               cnb_ref, dst_ref, buf_ref):
    n_slabs = GROUP_W // LANES
    for s in range(n_slabs):
        cols = slice(s * LANES, (s + 1) * LANES)
        tail = buf_ref[s, _rows2(CONV_QR, CONV_HALO), :]
        buf_ref[s, _rows2(0, CONV_HALO), :] = jnp.where(seq_start, 0.0, tail)
        buf_ref[s, _rows2(CONV_HALO, CONV_QR), :] = (
            ca_ref[:, cols].astype(F32) * _sigmoid(cb_ref[:, cols].astype(F32)))

    prev_acc = [None]

    def chunk(c):
        r0 = c * CONV_RC
        for s in range(n_slabs):
            cols = slice(s * LANES, (s + 1) * LANES)
            acc = jnp.broadcast_to(cbias_ref[:, cols], (CONV_RC, LANES))
            if unrolled and prev_acc[0] is not None:
                acc = acc + _derived_zero(prev_acc[0])
            for k in range(CONV_WIDTH):
                start = r0 + (CONV_HALO - (CONV_WIDTH - 1) + k)
                acc = acc + cw_ref[k:k + 1, cols] * buf_ref[s, _rows2(start, CONV_RC), :]
            prev_acc[0] = acc
            mu = jnp.mean(acc, axis=-1, keepdims=True)
            var = jnp.mean(jnp.square(acc - mu), axis=-1, keepdims=True)
            y = ((acc - mu) * lax.rsqrt(var + EPS)) * cng_ref[:, cols] + cnb_ref[:, cols]
            if unrolled:
                gate = cg_ref[r0:r0 + CONV_RC, cols].astype(F32)
            else:
                gate = cg_ref[pl.ds(pl.multiple_of(r0, CONV_RC), CONV_RC), cols].astype(F32)
            out_rows = pl.ds(pl.multiple_of(row0 + r0, CONV_RC), CONV_RC)
            dst_ref[out_rows, cols] = (_silu(y) * _silu(gate)).astype(dst_ref.dtype)

    if unrolled:
        for c in range(CONV_QR // CONV_RC):
            chunk(c)
    else:
        def body(c, carry):
            chunk(c)
            return carry
        lax.fori_loop(0, CONV_QR // CONV_RC, body, 0)


def _outproj_kernel(emit_x, steps_per_seq, ya_ref, yc_ref, ym_ref, w_ref, x_ref, ng_ref,
                    ca_ref, cb_ref, cg_ref, cw_ref, cbias_ref, cng_ref, cnb_ref, *rest):
    if emit_x:
        xo_ref, no_ref, yb_ref, buf_ref, ssq_ref, xrow_ref = rest
    else:
        xo_ref = None
        no_ref, yb_ref, buf_ref, ssq_ref = rest
        xrow_ref = no_ref
    i = pl.program_id(0)
    j = pl.program_id(1)
    step = i * OUT_NJ + j
    seq_start = (step % steps_per_seq) == 0
    conv_args = (ca_ref, cb_ref, cg_ref, cw_ref, cbias_ref, cng_ref, cnb_ref)
    row0 = j * CONV_QR

    @pl.when(i == 0)
    def _():
        @pl.when(j == 0)
        def _():
            for s in range(GROUP_W // LANES):
                buf_ref[s, _rows2(CONV_QR, CONV_HALO), :] = jnp.zeros((CONV_HALO, LANES), F32)

        _conv_rows(seq_start, row0, False, *conv_args, yb_ref.at[0], buf_ref)

    @pl.when(i > 0)
    def _():
        acc = x_ref[...]
        ys = ((1, yb_ref[(i - 1) % 2]), (0, ya_ref[...]), (2, yc_ref[...]), (3, ym_ref[...]))
        for g, y in ys:
            acc = acc + jnp.dot(y, w_ref[g * GROUP_W:(g + 1) * GROUP_W, :],
                                preferred_element_type=F32)
        if emit_x:
            xo_ref[...] = acc
        col0 = pl.multiple_of(j * OUT_BN, OUT_BN)
        xrow_ref[:, pl.ds(col0, OUT_BN)] = acc
        ssq_ref[j] = jnp.sum(acc * acc, axis=-1, keepdims=True)

        _conv_rows(seq_start, row0, True, *conv_args, yb_ref.at[i % 2], buf_ref)

        @pl.when(j == OUT_NJ - 1)
        def _():
            ssq = ssq_ref[0]
            for jj in range(1, OUT_NJ):
                ssq = ssq + ssq_ref[jj]
            rstd = lax.rsqrt(ssq * (1.0 / D_MODEL) + EPS)
            for jj in range(OUT_NJ):
                cols = slice(jj * OUT_BN, (jj + 1) * OUT_BN)
                no_ref[:, cols] = ((xrow_ref[:, cols] * rstd) * ng_ref[:, cols]).astype(no_ref.dtype)


def _outproj(ys, proj, seq_len, w_bf, x2d, next_g, conv_w, conv_b, conv_ng, conv_nb,
             emit_x, norm_dtype):
    m, d = x2d.shape
    k = w_bf.shape[0]
    nb = m // OUT_BM
    steps_per_seq = seq_len // CONV_QR
    last_q = m // CONV_QR - 1

    def prev(i):
        return jnp.maximum(i - 1, 0)

    y_spec = pl.BlockSpec((OUT_BM, GROUP_W), lambda i, j: (prev(i), 0))

    def conv_spec(col):
        return pl.BlockSpec((CONV_QR, GROUP_W),
                            lambda i, j: (jnp.minimum(i * OUT_NJ + j, last_q), col))

    def tile_idx(i, j):
        return (prev(i), jnp.where(i == 0, 0, j))

    row_full = pl.BlockSpec((1, d), lambda i, j: (0, 0))
    row_grp = pl.BlockSpec((1, GROUP_W), lambda i, j: (0, 0))
    in_specs = [y_spec, y_spec, y_spec,
                pl.BlockSpec((k, OUT_BN), lambda i, j: (0, j)),
                pl.BlockSpec((OUT_BM, OUT_BN), tile_idx),
                row_full,
                conv_spec(3), conv_spec(4), conv_spec(5),
                pl.BlockSpec((CONV_WIDTH, GROUP_W), lambda i, j: (0, 0)),
                row_grp, row_grp, row_grp]
    norm_spec = pl.BlockSpec((OUT_BM, d), lambda i, j: (prev(i), 0))
    norm_shape = jax.ShapeDtypeStruct((m, d), norm_dtype)
    if emit_x:
        out_specs = [pl.BlockSpec((OUT_BM, OUT_BN), tile_idx), norm_spec]
        out_shape = [jax.ShapeDtypeStruct((m, d), F32), norm_shape]
    else:
        out_specs = [norm_spec]
        out_shape = [norm_shape]
    scratch = [pltpu.VMEM((2, OUT_BM, GROUP_W), BF16),
               pltpu.VMEM((GROUP_W // LANES, 2 * (CONV_HALO + CONV_QR), LANES), F32),
               pltpu.VMEM((OUT_NJ, OUT_BM, 1), F32)]
    if emit_x:
        scratch.append(pltpu.VMEM((OUT_BM, d), F32))
    else:
        assert norm_dtype == F32
    ya, yc, ym = ys
    outs = pl.pallas_call(
        functools.partial(_outproj_kernel, emit_x, steps_per_seq),
        grid=(nb + 1, OUT_NJ),
        in_specs=in_specs,
        out_specs=out_specs,
        out_shape=out_shape,
        scratch_shapes=scratch,
        compiler_params=_params(("arbitrary", "arbitrary"), MM_VMEM_LIMIT),
        name="outproj",
    )(ya, yc, ym, w_bf, x2d, next_g.reshape(1, d), proj, proj, proj,
      conv_w, conv_b.reshape(1, -1), conv_ng.reshape(1, -1), conv_nb.reshape(1, -1))
    return outs if emit_x else (None, outs[0])


def kernel(x, mem, positions, norm_g, w_in, sgu_norm_g, sgu_norm_b, sgu_w, sgu_b,
           conv_w, conv_b, conv_norm_g, conv_norm_b, ret_norm_g, ret_norm_b,
           mem_norm_g, w_mem_kv, w_out, final_norm_g):
    bsz, s_len, d = x.shape
    depth = w_in.shape[0]
    assert d == D_MODEL and w_in.shape[2] == N_IN_SLICES * GROUP_W
    assert s_len % OUT_BM == 0 and s_len % MIX_ROWS == 0
    assert MIX_ROWS % SGU_CHUNK == 0 and MIX_ROWS % RET_CHUNK == 0
    assert GROUP_W // CONV_GROUPS == LANES and CONV_HALO >= CONV_WIDTH - 1
    assert CONV_QR % CONV_RC == 0 and CONV_QR >= CONV_HALO
    rows = bsz * s_len
    x2d = x.reshape(rows, d)
    mem2d = mem.reshape(bsz * MEM_LEN, d)

    half = RET_HEAD_DIM // 2
    inv_freq = jnp.power(ROPE_BASE, -jnp.arange(half, dtype=F32) / half).reshape(1, half)
    pos_col = positions.astype(F32).reshape(rows, 1)
    cos, sin = _rope_tables(pos_col, inv_freq)
    log_g = jnp.log(1.0 - jnp.power(2.0, -5.0 - jnp.arange(RET_HEADS, dtype=F32)))

    h = _rmsnorm(x2d, norm_g[0], BF16)
    out = None
    for l in range(depth):
        last = l == depth - 1
        proj, w_out_bf = _inproj(h, w_in, l, BF16, side_stack=w_out)

        bias_full = jnp.repeat(sgu_b[l].T, GROUP_W // SGU_HEADS, axis=1)
        y_a = _sgu(proj, sgu_norm_g[l], sgu_norm_b[l], sgu_w[l], bias_full)
        y_c = _retention(proj, s_len, log_g, cos, sin, ret_norm_g[l], ret_norm_b[l])

        mem_n = _rmsnorm(mem2d, mem_norm_g[l], BF16)
        kv = _inproj(mem_n, w_mem_kv, l, BF16)
        y_m = _memattn(proj, kv, s_len)

        next_g = final_norm_g if last else norm_g[l + 1]
        x2d, normed = _outproj((y_a, y_c, y_m), proj, s_len, w_out_bf, x2d, next_g,
                               conv_w[l], conv_b[l], conv_norm_g[l], conv_norm_b[l],
                               emit_x=not last, norm_dtype=F32 if last else BF16)
        if last:
            out = normed
        else:
            h = normed
    return out.reshape(bsz, s_len, d)
```

```python
import functools

import jax
import jax.numpy as jnp
from jax import lax
from jax.experimental import pallas as pl
from jax.experimental.pallas import tpu as pltpu

F32 = jnp.float32
BF16 = jnp.bfloat16

D_MODEL = 4096
GROUP_W = 1024
N_IN_SLICES = 12
SGU_CHUNK = 128
SGU_HEADS = 8
CONV_WIDTH = 31
CONV_GROUPS = 8
RET_HEADS = 4
RET_HEAD_DIM = 256
MEM_LEN = 256
MEM_HEADS = 4
MEM_HEAD_DIM = 256
ROPE_BASE = 10000.0
EPS = 1e-6

V7X_VMEM_BYTES = 64 * 1024 * 1024
LANES = 128

NORM_ROWS = 512
IN_BM = 1024
IN_BN = 1024
IN_MIN_ROW_TILES = 4
IN_STAGE_BYTES = 8 * 1024 * 1024
OUT_BM = 512
OUT_BN = 1024
OUT_NJ = D_MODEL // OUT_BN
CONV_QR = OUT_BM // OUT_NJ
CONV_RC = 16
CONV_HALO = 32
NORM_RC = 16
MIX_ROWS = 1024
RET_CHUNK = 256
MM_VMEM_LIMIT = 58 * 1024 * 1024
MIX_VMEM_LIMIT = 48 * 1024 * 1024


def _sigmoid(x):
    return 1.0 / (1.0 + jnp.exp(-x))


def _silu(x):
    return x * _sigmoid(x)


def _params(semantics, vmem_limit):
    return pltpu.CompilerParams(dimension_semantics=semantics, vmem_limit_bytes=vmem_limit)


def _rmsnorm_kernel(x_ref, g_ref, o_ref):
    x = x_ref[...]
    ms = jnp.mean(x * x, axis=-1, keepdims=True)
    o_ref[...] = ((x * lax.rsqrt(ms + EPS)) * g_ref[...]).astype(o_ref.dtype)


def _rmsnorm(x2d, g, out_dtype):
    rows, d = x2d.shape
    return pl.pallas_call(
        _rmsnorm_kernel,
        grid=(rows // NORM_ROWS,),
        in_specs=[pl.BlockSpec((NORM_ROWS, d), lambda i: (i, 0)),
                  pl.BlockSpec((1, d), lambda i: (0, 0))],
        out_specs=pl.BlockSpec((NORM_ROWS, d), lambda i: (i, 0)),
        out_shape=jax.ShapeDtypeStruct((rows, d), out_dtype),
        compiler_params=_params(("parallel",), MIX_VMEM_LIMIT),
        name="rmsnorm",
    )(x2d, g.reshape(1, d))


def _inproj_kernel(layer, n_col_blocks, n_row_tiles, side_cols, a_ref, w_hbm, *rest):
    j = pl.program_id(0)
    i = pl.program_id(1)
    if side_cols:
        side_ref, o_ref, side_out_ref, wb_ref, stage_ref, sem = rest

        @pl.when(j < side_cols)
        def _():
            side_out_ref[...] = side_ref[...].astype(BF16)
    else:
        o_ref, wb_ref, stage_ref, sem = rest
    n_slots, kc, _ = stage_ref.shape

    def chunk_copy(col_block, c, slot):
        rows = pl.ds(pl.multiple_of(c * kc, kc), kc)
        cols = pl.ds(pl.multiple_of(col_block * IN_BN, IN_BN), IN_BN)
        return pltpu.make_async_copy(w_hbm.at[layer, rows, cols], stage_ref.at[slot], sem.at[slot])

    @pl.when((j == 0) & (i == 0))
    def _():
        copies = [chunk_copy(0, c, c % n_slots) for c in range(n_row_tiles)]
        for c in range(min(n_slots, n_row_tiles)):
            copies[c].start()
        for c in range(n_row_tiles):
            copies[c].wait()
            wb_ref[0, c * kc:(c + 1) * kc, :] = stage_ref[c % n_slots].astype(BF16)
            if c + n_slots < n_row_tiles:
                copies[c + n_slots].start()

        if n_col_blocks > 1:
            chunk_copy(1, 0, 0).start()

    o_ref[...] = jnp.dot(a_ref[...], wb_ref[j % 2], preferred_element_type=F32).astype(o_ref.dtype)

    @pl.when(j + 1 < n_col_blocks)
    def _():
        chunk_copy(j + 1, i, 0).wait()
        rows = pl.ds(pl.multiple_of(i * kc, kc), kc)
        wb_ref[(j + 1) % 2, rows, :] = stage_ref[0].astype(BF16)

        @pl.when(i + 1 < n_row_tiles)
        def _():
            chunk_copy(j + 1, i + 1, 0).start()

        @pl.when((i + 1 == n_row_tiles) & (j + 2 < n_col_blocks))
        def _():
            chunk_copy(j + 2, 0, 0).start()


def _inproj(a, w_stack, layer, out_dtype, side_stack=None):
    m, k = a.shape
    n = w_stack.shape[2]
    bm = min(IN_BM, m // IN_MIN_ROW_TILES)
    n_row_tiles = m // bm
    n_col_blocks = n // IN_BN
    kc = k // n_row_tiles
    n_slots = max(2, min(4, IN_STAGE_BYTES // (kc * IN_BN * 4)))
    with_side = side_stack is not None
    side_cols = side_stack.shape[2] // IN_BN if with_side else 0
    kernel = functools.partial(_inproj_kernel, layer, n_col_blocks, n_row_tiles, side_cols)
    in_specs = [pl.BlockSpec((bm, k), lambda j, i: (i, 0)), pl.BlockSpec(memory_space=pl.ANY)]
    out_specs = [pl.BlockSpec((bm, IN_BN), lambda j, i: (i, j))]
    out_shape = [jax.ShapeDtypeStruct((m, n), out_dtype)]
    args = [a, w_stack]
    if with_side:
        _, sr, sc = side_stack.shape
        assert sr % n_row_tiles == 0 and sc % IN_BN == 0 and 0 < side_cols <= n_col_blocks

        def side_idx(j, i):
            return (jnp.where(j < side_cols, i, n_row_tiles - 1), jnp.minimum(j, side_cols - 1))

        in_specs.append(pl.BlockSpec((None, sr // n_row_tiles, IN_BN),
                                     lambda j, i: (layer,) + side_idx(j, i)))
        out_specs.append(pl.BlockSpec((sr // n_row_tiles, IN_BN), side_idx))
        out_shape.append(jax.ShapeDtypeStruct((sr, sc), BF16))
        args.append(side_stack)
    outs = pl.pallas_call(
        kernel,
        grid=(n_col_blocks, n_row_tiles),
        in_specs=in_specs,
        out_specs=out_specs,
        out_shape=out_shape,
        scratch_shapes=[pltpu.VMEM((2, k, IN_BN), BF16),
                        pltpu.VMEM((n_slots, kc, IN_BN), F32),
                        pltpu.SemaphoreType.DMA((n_slots,))],
        compiler_params=_params(("arbitrary", "arbitrary"), MM_VMEM_LIMIT),
        name="inproj",
    )(*args)
    return outs if with_side else outs[0]


def _rope_kernel(pos_ref, freq_ref, cos_ref, sin_ref):
    ang = pos_ref[...] * freq_ref[...]
    cos_ref[...] = jnp.cos(ang)
    sin_ref[...] = jnp.sin(ang)


def _rope_tables(pos_col, inv_freq):
    rows = pos_col.shape[0]
    half = inv_freq.shape[1]
    spec = pl.BlockSpec((MIX_ROWS, half), lambda i: (i, 0))
    return pl.pallas_call(
        _rope_kernel,
        grid=(rows // MIX_ROWS,),
        in_specs=[pl.BlockSpec((MIX_ROWS, 1), lambda i: (i, 0)),
                  pl.BlockSpec((1, half), lambda i: (0, 0))],
        out_specs=[spec, spec],
        out_shape=[jax.ShapeDtypeStruct((rows, half), F32)] * 2,
        compiler_params=_params(("parallel",), MIX_VMEM_LIMIT),
        name="rope_tables",
    )(pos_col, inv_freq)


def _slice_spec(rows, col_block):
    return pl.BlockSpec((rows, GROUP_W), lambda i: (i, col_block))


def _row_spec(n):
    return pl.BlockSpec((1, n), lambda i: (0, 0))


def _sgu_kernel(u_ref, v_ref, g_ref, ng_ref, nb_ref, w_ref, bias_ref, o_ref, vn_ref):
    v = v_ref[...].astype(F32)
    mu = jnp.mean(v, axis=-1, keepdims=True)
    var = jnp.mean(jnp.square(v - mu), axis=-1, keepdims=True)
    vn = ((v - mu) * lax.rsqrt(var + EPS)) * ng_ref[...] + nb_ref[...]
    vn_ref[...] = vn.astype(BF16)

    t = lax.broadcasted_iota(jnp.int32, (SGU_CHUNK, SGU_CHUNK), 0)
    s = lax.broadcasted_iota(jnp.int32, (SGU_CHUNK, SGU_CHUNK), 1)
    causal = t >= s
    for h in range(SGU_HEADS):
        w = jnp.where(causal, w_ref[h], 0.0).astype(BF16)
        cols = slice(h * SGU_CHUNK, (h + 1) * SGU_CHUNK)
        for n in range(MIX_ROWS // SGU_CHUNK):
            rows = slice(n * SGU_CHUNK, (n + 1) * SGU_CHUNK)
            mix = jnp.dot(w, vn_ref[rows, cols], preferred_element_type=F32) + bias_ref[:, cols]
            u = u_ref[rows, cols].astype(F32)
            gate = g_ref[rows, cols].astype(F32)
            o_ref[rows, cols] = ((u * mix) * _silu(gate)).astype(o_ref.dtype)


def _sgu(proj, ng, nb, w, bias_full):
    rows = proj.shape[0]
    return pl.pallas_call(
        _sgu_kernel,
        grid=(rows // MIX_ROWS,),
        in_specs=[_slice_spec(MIX_ROWS, 0), _slice_spec(MIX_ROWS, 1), _slice_spec(MIX_ROWS, 2),
                  _row_spec(GROUP_W), _row_spec(GROUP_W),
                  pl.BlockSpec((SGU_HEADS, SGU_CHUNK, SGU_CHUNK), lambda i: (0, 0, 0)),
                  pl.BlockSpec((SGU_CHUNK, GROUP_W), lambda i: (0, 0))],
        out_specs=pl.BlockSpec((MIX_ROWS, GROUP_W), lambda i: (i, 0)),
        out_shape=jax.ShapeDtypeStruct((rows, GROUP_W), BF16),
        scratch_shapes=[pltpu.VMEM((MIX_ROWS, GROUP_W), BF16)],
        compiler_params=_params(("parallel",), MIX_VMEM_LIMIT),
        name="sgu",
    )(proj, proj, proj, ng.reshape(1, -1), nb.reshape(1, -1), w, bias_full)


def _retention_kernel(steps_per_seq, logg_ref, q_ref, k_ref, v_ref, g_ref, cos_ref, sin_ref,
                      ng_ref, nb_ref, o_ref, state_ref, decay_ref):
    i = pl.program_id(0)

    @pl.when((i % steps_per_seq) == 0)
    def _():
        state_ref[...] = jnp.zeros_like(state_ref)

    c = RET_CHUNK
    half = RET_HEAD_DIM // 2
    scale = RET_HEAD_DIM ** -0.5
    t_row = lax.broadcasted_iota(jnp.int32, (c, c), 0)
    s_col = lax.broadcasted_iota(jnp.int32, (c, c), 1)
    diff = (t_row - s_col).astype(F32)
    t_vec = lax.broadcasted_iota(jnp.int32, (c, 1), 0).astype(F32)
    for h in range(RET_HEADS):
        log_g = logg_ref[h]
        decay_ref[h] = jnp.where(diff >= 0, jnp.exp(log_g * jnp.maximum(diff, 0.0)), 0.0)

    def chunk(ci, carry):
        rows = pl.ds(pl.multiple_of(ci * c, c), c)
        cos = cos_ref[rows, :]
        sin = sin_ref[rows, :]

        def rot(ref, lo):
            x1 = ref[rows, lo:lo + half].astype(F32)
            x2 = ref[rows, lo + half:lo + 2 * half].astype(F32)
            return x1 * cos - x2 * sin, x2 * cos + x1 * sin

        for h in range(RET_HEADS):
            lo = h * RET_HEAD_DIM
            cols = slice(lo, lo + RET_HEAD_DIM)
            log_g = logg_ref[h]
            q1, q2 = rot(q_ref, lo)
            k1, k2 = rot(k_ref, lo)
            q = jnp.concatenate([q1, q2], axis=-1)
            k = jnp.concatenate([k1, k2], axis=-1) * scale
            v = v_ref[rows, cols].astype(BF16)

            k_decay = jnp.exp(log_g * ((c - 1.0) - t_vec))
            q_decay = jnp.exp(log_g * (t_vec + 1.0))
            chunk_decay = jnp.exp(log_g * float(c))

            scores = lax.dot_general(q.astype(BF16), k.astype(BF16), (((1,), (1,)), ((), ())),
                                     preferred_element_type=F32) * decay_ref[h]
            intra = jnp.dot(scores.astype(BF16), v, preferred_element_type=F32)
            state = state_ref[h]
            cross = jnp.dot((q * q_decay).astype(BF16), state.astype(BF16),
                            preferred_element_type=F32)
            kv = lax.dot_general((k * k_decay).astype(BF16), v, (((0,), (0,)), ((), ())),
                                 preferred_element_type=F32)
            state_ref[h] = chunk_decay * state + kv

            r = intra + cross
            mu = jnp.mean(r, axis=-1, keepdims=True)
            var = jnp.mean(jnp.square(r - mu), axis=-1, keepdims=True)
            y = ((r - mu) * lax.rsqrt(var + EPS)) * ng_ref[:, cols] + nb_ref[:, cols]
            gate = g_ref[rows, cols].astype(F32)
            o_ref[rows, cols] = (y * _silu(gate)).astype(o_ref.dtype)
        return carry

    lax.fori_loop(0, MIX_ROWS // c, chunk, 0)


def _retention(proj, seq_len, log_g, cos, sin, ng, nb):
    rows = proj.shape[0]
    half = RET_HEAD_DIM // 2
    kernel = functools.partial(_retention_kernel, seq_len // MIX_ROWS)
    tab_spec = pl.BlockSpec((MIX_ROWS, half), lambda i: (i, 0))
    return pl.pallas_call(
        kernel,
        grid=(rows // MIX_ROWS,),
        in_specs=[pl.BlockSpec(memory_space=pltpu.SMEM),
                  _slice_spec(MIX_ROWS, 6), _slice_spec(MIX_ROWS, 7), _slice_spec(MIX_ROWS, 8),
                  _slice_spec(MIX_ROWS, 9), tab_spec, tab_spec,
                  _row_spec(GROUP_W), _row_spec(GROUP_W)],
        out_specs=pl.BlockSpec((MIX_ROWS, GROUP_W), lambda i: (i, 0)),
        out_shape=jax.ShapeDtypeStruct((rows, GROUP_W), BF16),
        scratch_shapes=[pltpu.VMEM((RET_HEADS, RET_HEAD_DIM, RET_HEAD_DIM), F32),
                        pltpu.VMEM((RET_HEADS, RET_CHUNK, RET_CHUNK), F32)],
        compiler_params=_params(("arbitrary",), MIX_VMEM_LIMIT),
        name="retention",
    )(log_g, proj, proj, proj, proj, cos, sin, ng.reshape(1, -1), nb.reshape(1, -1))


def _memattn_kernel(q_ref, g_ref, k_ref, v_ref, o_ref):
    scale = MEM_HEAD_DIM ** -0.5
    for h in range(MEM_HEADS):
        cols = slice(h * MEM_HEAD_DIM, (h + 1) * MEM_HEAD_DIM)
        s = lax.dot_general(q_ref[:, cols], k_ref[:, cols], (((1,), (1,)), ((), ())),
                            preferred_element_type=F32) * scale
        m = jnp.max(s, axis=-1, keepdims=True)
        p = jnp.exp(s - m)
        denom = jnp.sum(p, axis=-1, keepdims=True)
        p = p / denom
        o = jnp.dot(p.astype(BF16), v_ref[:, cols], preferred_element_type=F32)
        gate = g_ref[:, cols].astype(F32)
        o_ref[:, cols] = (o * _silu(gate)).astype(o_ref.dtype)


def _memattn(proj, kv, seq_len):
    rows = proj.shape[0]
    steps_per_seq = seq_len // MIX_ROWS
    return pl.pallas_call(
        _memattn_kernel,
        grid=(rows // MIX_ROWS,),
        in_specs=[_slice_spec(MIX_ROWS, 10), _slice_spec(MIX_ROWS, 11),
                  pl.BlockSpec((MEM_LEN, GROUP_W), lambda i: (i // steps_per_seq, 0)),
                  pl.BlockSpec((MEM_LEN, GROUP_W), lambda i: (i // steps_per_seq, 1))],
        out_specs=pl.BlockSpec((MIX_ROWS, GROUP_W), lambda i: (i, 0)),
        out_shape=jax.ShapeDtypeStruct((rows, GROUP_W), BF16),
        compiler_params=_params(("parallel",), MIX_VMEM_LIMIT),
        name="memattn",
    )(proj, proj, kv, kv)


def _rows2(start, n):
    return pl.ds(2 * start, n, stride=2)


def _derived_zero(x):
    bits = lax.bitcast_convert_type(x, jnp.uint32)
    bits = lax.shift_right_logical(lax.shift_right_logical(bits, jnp.uint32(16)), jnp.uint32(16))
    return lax.bitcast_convert_type(bits, F32)


def _conv_rows(seq_start, row0, unrolled, ca_ref, cb_ref, cg_ref, cw_ref, cbias_ref, cng_ref,
               cnb_ref, dst_ref, buf_ref):
    n_slabs = GROUP_W // LANES
    for s in range(n_slabs):
        cols = slice(s * LANES, (s + 1) * LANES)
        tail = buf_ref[s, _rows2(CONV_QR, CONV_HALO), :]
        buf_ref[s, _rows2(0, CONV_HALO), :] = jnp.where(seq_start, 0.0, tail)
        buf_ref[s, _rows2(CONV_HALO, CONV_QR), :] = (
            ca_ref[:, cols].astype(F32) * _sigmoid(cb_ref[:, cols].astype(F32)))

    prev_acc = [None]

    def chunk(c):
        r0 = c * CONV_RC
        for s in range(n_slabs):
            cols = slice(s * LANES, (s + 1) * LANES)
            acc = jnp.broadcast_to(cbias_ref[:, cols], (CONV_RC, LANES))
            if unrolled and prev_acc[0] is not None:
                acc = acc + _derived_zero(prev_acc[0])
            for k in range(CONV_WIDTH):
                start = r0 + (CONV_HALO - (CONV_WIDTH - 1) + k)
                acc = acc + cw_ref[k:k + 1, cols] * buf_ref[s, _rows2(start, CONV_RC), :]
            prev_acc[0] = acc
            mu = jnp.mean(acc, axis=-1, keepdims=True)
            var = jnp.mean(jnp.square(acc - mu), axis=-1, keepdims=True)
            y = ((acc - mu) * lax.rsqrt(var + EPS)) * cng_ref[:, cols] + cnb_ref[:, cols]
            if unrolled:
                gate = cg_ref[r0:r0 + CONV_RC, cols].astype(F32)
            else:
                gate = cg_ref[pl.ds(pl.multiple_of(r0, CONV_RC), CONV_RC), cols].astype(F32)
            out_rows = pl.ds(pl.multiple_of(row0 + r0, CONV_RC), CONV_RC)
            dst_ref[out_rows, cols] = (_silu(y) * _silu(gate)).astype(dst_ref.dtype)

    if unrolled:
        for c in range(CONV_QR // CONV_RC):
            chunk(c)
    else:
        def body(c, carry):
            chunk(c)
            return carry
        lax.fori_loop(0, CONV_QR // CONV_RC, body, 0)


def _outproj_kernel(emit_x, steps_per_seq, ya_ref, yc_ref, ym_ref, w_ref, x_ref, ng_ref,
                    ca_ref, cb_ref, cg_ref, cw_ref, cbias_ref, cng_ref, cnb_ref, *rest):
    if emit_x:
        xo_ref, no_ref, yb_ref, buf_ref, ssq_ref, xrow_ref = rest
    else:
        xo_ref = None
        no_ref, yb_ref, buf_ref, ssq_ref = rest
        xrow_ref = no_ref
    i = pl.program_id(0)
    j = pl.program_id(1)
    step = i * OUT_NJ + j
    seq_start = (step % steps_per_seq) == 0
    conv_args = (ca_ref, cb_ref, cg_ref, cw_ref, cbias_ref, cng_ref, cnb_ref)
    row0 = j * CONV_QR

    @pl.when(i == 0)
    def _():
        @pl.when(j == 0)
        def _():
            for s in range(GROUP_W // LANES):
                buf_ref[s, _rows2(CONV_QR, CONV_HALO), :] = jnp.zeros((CONV_HALO, LANES), F32)

        _conv_rows(seq_start, row0, False, *conv_args, yb_ref.at[0], buf_ref)

    @pl.when(i > 0)
    def _():
        acc = x_ref[...]
        ys = ((1, yb_ref[(i - 1) % 2]), (0, ya_ref[...]), (2, yc_ref[...]), (3, ym_ref[...]))
        for g, y in ys:
            acc = acc + jnp.dot(y, w_ref[g * GROUP_W:(g + 1) * GROUP_W, :],
                                preferred_element_type=F32)
        if emit_x:
            xo_ref[...] = acc
        col0 = pl.multiple_of(j * OUT_BN, OUT_BN)
        xrow_ref[:, pl.ds(col0, OUT_BN)] = acc
        ssq_ref[j] = jnp.sum(acc * acc, axis=-1, keepdims=True)

        _conv_rows(seq_start, row0, True, *conv_args, yb_ref.at[i % 2], buf_ref)

        @pl.when(j == OUT_NJ - 1)
        def _():
            ssq = ssq_ref[0]
            for jj in range(1, OUT_NJ):
                ssq = ssq + ssq_ref[jj]
            ssq_ref[0] = lax.rsqrt(ssq * (1.0 / D_MODEL) + EPS)

            for c in range(OUT_BM // NORM_RC):
                rows = slice(c * NORM_RC, (c + 1) * NORM_RC)
                rstd = ssq_ref[0, rows, :]
                for jj in range(OUT_NJ):
                    cols = slice(jj * OUT_BN, (jj + 1) * OUT_BN)
                    no_ref[rows, cols] = ((xrow_ref[rows, cols] * rstd)
                                          * ng_ref[:, cols]).astype(no_ref.dtype)


def _outproj(ys, proj, seq_len, w_bf, x2d, next_g, conv_w, conv_b, conv_ng, conv_nb,
             emit_x, norm_dtype):
    m, d = x2d.shape
    k = w_bf.shape[0]
    nb = m // OUT_BM
    steps_per_seq = seq_len // CONV_QR
    last_q = m // CONV_QR - 1

    def prev(i):
        return jnp.maximum(i - 1, 0)

    y_spec = pl.BlockSpec((OUT_BM, GROUP_W), lambda i, j: (prev(i), 0))

    def conv_spec(col):
        return pl.BlockSpec((CONV_QR, GROUP_W),
                            lambda i, j: (jnp.minimum(i * OUT_NJ + j, last_q), col))

    def tile_idx(i, j):
        return (prev(i), jnp.where(i == 0, 0, j))

    row_full = pl.BlockSpec((1, d), lambda i, j: (0, 0))
    row_grp = pl.BlockSpec((1, GROUP_W), lambda i, j: (0, 0))
    in_specs = [y_spec, y_spec, y_spec,
                pl.BlockSpec((k, OUT_BN), lambda i, j: (0, j)),
                pl.BlockSpec((OUT_BM, OUT_BN), tile_idx),
                row_full,
                conv_spec(3), conv_spec(4), conv_spec(5),
                pl.BlockSpec((CONV_WIDTH, GROUP_W), lambda i, j: (0, 0)),
                row_grp, row_grp, row_grp]
    norm_spec = pl.BlockSpec((OUT_BM, d), lambda i, j: (prev(i), 0))
    norm_shape = jax.ShapeDtypeStruct((m, d), norm_dtype)
    if emit_x:
        out_specs = [pl.BlockSpec((OUT_BM, OUT_BN), tile_idx), norm_spec]
        out_shape = [jax.ShapeDtypeStruct((m, d), F32), norm_shape]
    else:
        out_specs = [norm_spec]
        out_shape = [norm_shape]
    scratch = [pltpu.VMEM((2, OUT_BM, GROUP_W), BF16),
               pltpu.VMEM((GROUP_W // LANES, 2 * (CONV_HALO + CONV_QR), LANES), F32),
               pltpu.VMEM((OUT_NJ, OUT_BM, 1), F32)]
    if emit_x:
        scratch.append(pltpu.VMEM((OUT_BM, d), F32))
    else:
        assert norm_dtype == F32
    ya, yc, ym = ys
    outs = pl.pallas_call(
        functools.partial(_outproj_kernel, emit_x, steps_per_seq),
        grid=(nb + 1, OUT_NJ),
        in_specs=in_specs,
        out_specs=out_specs,
        out_shape=out_shape,
        scratch_shapes=scratch,
        compiler_params=_params(("arbitrary", "arbitrary"), MM_VMEM_LIMIT),
        name="outproj",
    )(ya, yc, ym, w_bf, x2d, next_g.reshape(1, d), proj, proj, proj,
      conv_w, conv_b.reshape(1, -1), conv_ng.reshape(1, -1), conv_nb.reshape(1, -1))
    return outs if emit_x else (None, outs[0])


def kernel(x, mem, positions, norm_g, w_in, sgu_norm_g, sgu_norm_b, sgu_w, sgu_b,
           conv_w, conv_b, conv_norm_g, conv_norm_b, ret_norm_g, ret_norm_b,
           mem_norm_g, w_mem_kv, w_out, final_norm_g):
    bsz, s_len, d = x.shape
    depth = w_in.shape[0]
    assert d == D_MODEL and w_in.shape[2] == N_IN_SLICES * GROUP_W
    assert s_len % OUT_BM == 0 and s_len % MIX_ROWS == 0
    assert MIX_ROWS % SGU_CHUNK == 0 and MIX_ROWS % RET_CHUNK == 0
    assert GROUP_W // CONV_GROUPS == LANES and CONV_HALO >= CONV_WIDTH - 1
    assert CONV_QR % CONV_RC == 0 and CONV_QR >= CONV_HALO
    rows = bsz * s_len
    x2d = x.reshape(rows, d)
    mem2d = mem.reshape(bsz * MEM_LEN, d)

    half = RET_HEAD_DIM // 2
    inv_freq = jnp.power(ROPE_BASE, -jnp.arange(half, dtype=F32) / half).reshape(1, half)
    pos_col = positions.astype(F32).reshape(rows, 1)
    cos, sin = _rope_tables(pos_col, inv_freq)
    log_g = jnp.log(1.0 - jnp.power(2.0, -5.0 - jnp.arange(RET_HEADS, dtype=F32)))

    h = _rmsnorm(x2d, norm_g[0], BF16)
    out = None
    for l in range(depth):
        last = l == depth - 1
        proj, w_out_bf = _inproj(h, w_in, l, BF16, side_stack=w_out)

        bias_full = jnp.repeat(sgu_b[l].T, GROUP_W // SGU_HEADS, axis=1)
        y_a = _sgu(proj, sgu_norm_g[l], sgu_norm_b[l], sgu_w[l], bias_full)
        y_c = _retention(proj, s_len, log_g, cos, sin, ret_norm_g[l], ret_norm_b[l])

        mem_n = _rmsnorm(mem2d, mem_norm_g[l], BF16)
        kv = _inproj(mem_n, w_mem_kv, l, BF16)
        y_m = _memattn(proj, kv, s_len)

        next_g = final_norm_g if last else norm_g[l + 1]
        x2d, normed = _outproj((y_a, y_c, y_m), proj, s_len, w_out_bf, x2d, next_g,
                               conv_w[l], conv_b[l], conv_norm_g[l], conv_norm_b[l],
                               emit_x=not last, norm_dtype=F32 if last else BF16)
        if last:
            out = normed
        else:
            h = normed
    return out.reshape(bsz, s_len, d)
```

```python
import functools

import jax
import jax.numpy as jnp
from jax import lax
from jax.experimental import pallas as pl
from jax.experimental.pallas import tpu as pltpu

F32 = jnp.float32
BF16 = jnp.bfloat16

D_MODEL = 4096
GROUP_W = 1024
N_IN_SLICES = 12
SGU_CHUNK = 128
SGU_HEADS = 8
CONV_WIDTH = 31
CONV_GROUPS = 8
RET_HEADS = 4
RET_HEAD_DIM = 256
MEM_LEN = 256
MEM_HEADS = 4
MEM_HEAD_DIM = 256
ROPE_BASE = 10000.0
EPS = 1e-6

V7X_VMEM_BYTES = 64 * 1024 * 1024
LANES = 128

NORM_ROWS = 512
FIRST_BM = 512
IN_BM = 1024
IN_BN = 1024
IN_MIN_ROW_TILES = 4
IN_STAGE_BYTES = 8 * 1024 * 1024
OUT_BM = 512
OUT_BN = 1024
OUT_NJ = D_MODEL // OUT_BN
CONV_QR = OUT_BM // OUT_NJ
CONV_RC = 16
CONV_HALO = 32
NORM_RC = 16
MIX_ROWS = 1024
RET_CHUNK = 256
MM_VMEM_LIMIT = 58 * 1024 * 1024
MIX_VMEM_LIMIT = 48 * 1024 * 1024


def _sigmoid(x):
    return 1.0 / (1.0 + jnp.exp(-x))


def _silu(x):
    return x * _sigmoid(x)


def _params(semantics, vmem_limit):
    return pltpu.CompilerParams(dimension_semantics=semantics, vmem_limit_bytes=vmem_limit)


def _rmsnorm_kernel(x_ref, g_ref, o_ref):
    x = x_ref[...]
    ms = jnp.mean(x * x, axis=-1, keepdims=True)
    o_ref[...] = ((x * lax.rsqrt(ms + EPS)) * g_ref[...]).astype(o_ref.dtype)


def _rmsnorm(x2d, g, out_dtype):
    rows, d = x2d.shape
    return pl.pallas_call(
        _rmsnorm_kernel,
        grid=(rows // NORM_ROWS,),
        in_specs=[pl.BlockSpec((NORM_ROWS, d), lambda i: (i, 0)),
                  pl.BlockSpec((1, d), lambda i: (0, 0))],
        out_specs=pl.BlockSpec((NORM_ROWS, d), lambda i: (i, 0)),
        out_shape=jax.ShapeDtypeStruct((rows, d), out_dtype),
        compiler_params=_params(("parallel",), MIX_VMEM_LIMIT),
        name="rmsnorm",
    )(x2d, g.reshape(1, d))


def _norm_proj_kernel(x_ref, g_ref, w_ref, h_ref, p_ref):
    for c in range(FIRST_BM // NORM_RC):
        rows = slice(c * NORM_RC, (c + 1) * NORM_RC)
        x = x_ref[rows, :]
        ms = jnp.mean(x * x, axis=-1, keepdims=True)
        h_ref[rows, :] = ((x * lax.rsqrt(ms + EPS)) * g_ref[...]).astype(h_ref.dtype)
    p_ref[...] = jnp.dot(h_ref[...], w_ref[...], preferred_element_type=F32).astype(p_ref.dtype)


def _norm_proj(x2d, g, w_bf):
    rows, d = x2d.shape
    n = w_bf.shape[1]
    bm = min(FIRST_BM, rows)
    assert bm == FIRST_BM
    return pl.pallas_call(
        _norm_proj_kernel,
        grid=(rows // bm,),
        in_specs=[pl.BlockSpec((bm, d), lambda i: (i, 0)),
                  pl.BlockSpec((1, d), lambda i: (0, 0)),
                  pl.BlockSpec((d, n), lambda i: (0, 0))],
        out_specs=[pl.BlockSpec((bm, d), lambda i: (i, 0)),
                   pl.BlockSpec((bm, n), lambda i: (i, 0))],
        out_shape=[jax.ShapeDtypeStruct((rows, d), BF16),
                   jax.ShapeDtypeStruct((rows, n), BF16)],
        compiler_params=_params(("parallel",), MM_VMEM_LIMIT),
        name="norm_proj",
    )(x2d, g.reshape(1, d), w_bf)


def _inproj_kernel(layer, col_offset, n_col_blocks, n_row_tiles, side_cols, a_ref, w_hbm, *rest):
    j = pl.program_id(0)
    i = pl.program_id(1)
    if side_cols:
        side_ref, o_ref, side_out_ref, wb_ref, stage_ref, sem = rest

        @pl.when(j < side_cols)
        def _():
            side_out_ref[...] = side_ref[...].astype(BF16)
    else:
        o_ref, wb_ref, stage_ref, sem = rest
    n_slots, kc, _ = stage_ref.shape

    def chunk_copy(col_block, c, slot):
        rows = pl.ds(pl.multiple_of(c * kc, kc), kc)
        cols = pl.ds(pl.multiple_of((col_block + col_offset) * IN_BN, IN_BN), IN_BN)
        return pltpu.make_async_copy(w_hbm.at[layer, rows, cols], stage_ref.at[slot], sem.at[slot])

    @pl.when((j == 0) & (i == 0))
    def _():
        copies = [chunk_copy(0, c, c % n_slots) for c in range(n_row_tiles)]
        for c in range(min(n_slots, n_row_tiles)):
            copies[c].start()
        for c in range(n_row_tiles):
            copies[c].wait()
            wb_ref[0, c * kc:(c + 1) * kc, :] = stage_ref[c % n_slots].astype(BF16)
            if c + n_slots < n_row_tiles:
                copies[c + n_slots].start()

        if n_col_blocks > 1:
            chunk_copy(1, 0, 0).start()

    o_ref[...] = jnp.dot(a_ref[...], wb_ref[j % 2], preferred_element_type=F32).astype(o_ref.dtype)

    @pl.when(j + 1 < n_col_blocks)
    def _():
        chunk_copy(j + 1, i, 0).wait()
        rows = pl.ds(pl.multiple_of(i * kc, kc), kc)
        wb_ref[(j + 1) % 2, rows, :] = stage_ref[0].astype(BF16)

        @pl.when(i + 1 < n_row_tiles)
        def _():
            chunk_copy(j + 1, i + 1, 0).start()

        @pl.when((i + 1 == n_row_tiles) & (j + 2 < n_col_blocks))
        def _():
            chunk_copy(j + 2, 0, 0).start()


def _inproj(a, w_stack, layer, out_dtype, side_stack=None, col_offset=0):
    m, k = a.shape
    n = w_stack.shape[2] - col_offset * IN_BN
    bm = min(IN_BM, m // IN_MIN_ROW_TILES)
    n_row_tiles = m // bm
    n_col_blocks = n // IN_BN
    kc = k // n_row_tiles
    n_slots = max(2, min(4, IN_STAGE_BYTES // (kc * IN_BN * 4)))
    with_side = side_stack is not None
    side_cols = side_stack.shape[2] // IN_BN if with_side else 0
    kernel = functools.partial(_inproj_kernel, layer, col_offset, n_col_blocks, n_row_tiles,
                               side_cols)
    in_specs = [pl.BlockSpec((bm, k), lambda j, i: (i, 0)), pl.BlockSpec(memory_space=pl.ANY)]
    out_specs = [pl.BlockSpec((bm, IN_BN), lambda j, i: (i, j))]
    out_shape = [jax.ShapeDtypeStruct((m, n), out_dtype)]
    args = [a, w_stack]
    if with_side:
        _, sr, sc = side_stack.shape
        assert sr % n_row_tiles == 0 and sc % IN_BN == 0 and 0 < side_cols <= n_col_blocks

        def side_idx(j, i):
            return (jnp.where(j < side_cols, i, n_row_tiles - 1), jnp.minimum(j, side_cols - 1))

        in_specs.append(pl.BlockSpec((None, sr // n_row_tiles, IN_BN),
                                     lambda j, i: (layer,) + side_idx(j, i)))
        out_specs.append(pl.BlockSpec((sr // n_row_tiles, IN_BN), side_idx))
        out_shape.append(jax.ShapeDtypeStruct((sr, sc), BF16))
        args.append(side_stack)
    outs = pl.pallas_call(
        kernel,
        grid=(n_col_blocks, n_row_tiles),
        in_specs=in_specs,
        out_specs=out_specs,
        out_shape=out_shape,
        scratch_shapes=[pltpu.VMEM((2, k, IN_BN), BF16),
                        pltpu.VMEM((n_slots, kc, IN_BN), F32),
                        pltpu.SemaphoreType.DMA((n_slots,))],
        compiler_params=_params(("arbitrary", "arbitrary"), MM_VMEM_LIMIT),
        name="inproj",
    )(*args)
    return outs if with_side else outs[0]


def _rope_kernel(pos_ref, freq_ref, cos_ref, sin_ref):
    ang = pos_ref[...] * freq_ref[...]
    cos_ref[...] = jnp.cos(ang)
    sin_ref[...] = jnp.sin(ang)


def _rope_tables(pos_col, inv_freq):
    rows = pos_col.shape[0]
    half = inv_freq.shape[1]
    spec = pl.BlockSpec((MIX_ROWS, half), lambda i: (i, 0))
    return pl.pallas_call(
        _rope_kernel,
        grid=(rows // MIX_ROWS,),
        in_specs=[pl.BlockSpec((MIX_ROWS, 1), lambda i: (i, 0)),
                  pl.BlockSpec((1, half), lambda i: (0, 0))],
        out_specs=[spec, spec],
        out_shape=[jax.ShapeDtypeStruct((rows, half), F32)] * 2,
        compiler_params=_params(("parallel",), MIX_VMEM_LIMIT),
        name="rope_tables",
    )(pos_col, inv_freq)


def _slice_spec(rows, col_block):
    return pl.BlockSpec((rows, GROUP_W), lambda i: (i, col_block))


def _slices(src, ids, rows):
    arrs, cols = zip(*[src(s) for s in ids])
    return list(arrs), [_slice_spec(rows, c) for c in cols]


def _row_spec(n):
    return pl.BlockSpec((1, n), lambda i: (0, 0))


def _sgu_kernel(u_ref, v_ref, g_ref, ng_ref, nb_ref, w_ref, bias_ref, o_ref, vn_ref):
    v = v_ref[...].astype(F32)
    mu = jnp.mean(v, axis=-1, keepdims=True)
    var = jnp.mean(jnp.square(v - mu), axis=-1, keepdims=True)
    vn = ((v - mu) * lax.rsqrt(var + EPS)) * ng_ref[...] + nb_ref[...]
    vn_ref[...] = vn.astype(BF16)

    t = lax.broadcasted_iota(jnp.int32, (SGU_CHUNK, SGU_CHUNK), 0)
    s = lax.broadcasted_iota(jnp.int32, (SGU_CHUNK, SGU_CHUNK), 1)
    causal = t >= s
    for h in range(SGU_HEADS):
        w = jnp.where(causal, w_ref[h], 0.0).astype(BF16)
        cols = slice(h * SGU_CHUNK, (h + 1) * SGU_CHUNK)
        for n in range(MIX_ROWS // SGU_CHUNK):
            rows = slice(n * SGU_CHUNK, (n + 1) * SGU_CHUNK)
            mix = jnp.dot(w, vn_ref[rows, cols], preferred_element_type=F32) + bias_ref[:, cols]
            u = u_ref[rows, cols].astype(F32)
            gate = g_ref[rows, cols].astype(F32)
            o_ref[rows, cols] = ((u * mix) * _silu(gate)).astype(o_ref.dtype)


def _sgu(src, ng, nb, w, bias_full):
    arrs, specs = _slices(src, (0, 1, 2), MIX_ROWS)
    rows = arrs[0].shape[0]
    return pl.pallas_call(
        _sgu_kernel,
        grid=(rows // MIX_ROWS,),
        in_specs=specs + [
                  _row_spec(GROUP_W), _row_spec(GROUP_W),
                  pl.BlockSpec((SGU_HEADS, SGU_CHUNK, SGU_CHUNK), lambda i: (0, 0, 0)),
                  pl.BlockSpec((SGU_CHUNK, GROUP_W), lambda i: (0, 0))],
        out_specs=pl.BlockSpec((MIX_ROWS, GROUP_W), lambda i: (i, 0)),
        out_shape=jax.ShapeDtypeStruct((rows, GROUP_W), BF16),
        scratch_shapes=[pltpu.VMEM((MIX_ROWS, GROUP_W), BF16)],
        compiler_params=_params(("parallel",), MIX_VMEM_LIMIT),
        name="sgu",
    )(*arrs, ng.reshape(1, -1), nb.reshape(1, -1), w, bias_full)


def _retention_kernel(steps_per_seq, logg_ref, q_ref, k_ref, v_ref, g_ref, cos_ref, sin_ref,
                      ng_ref, nb_ref, o_ref, state_ref, decay_ref):
    i = pl.program_id(0)

    @pl.when((i % steps_per_seq) == 0)
    def _():
        state_ref[...] = jnp.zeros_like(state_ref)

    c = RET_CHUNK
    half = RET_HEAD_DIM // 2
    scale = RET_HEAD_DIM ** -0.5
    t_row = lax.broadcasted_iota(jnp.int32, (c, c), 0)
    s_col = lax.broadcasted_iota(jnp.int32, (c, c), 1)
    diff = (t_row - s_col).astype(F32)
    t_vec = lax.broadcasted_iota(jnp.int32, (c, 1), 0).astype(F32)
    for h in range(RET_HEADS):
        log_g = logg_ref[h]
        decay_ref[h] = jnp.where(diff >= 0, jnp.exp(log_g * jnp.maximum(diff, 0.0)), 0.0)

    def chunk(ci, carry):
        rows = pl.ds(pl.multiple_of(ci * c, c), c)
        cos = cos_ref[rows, :]
        sin = sin_ref[rows, :]

        def rot(ref, lo):
            x1 = ref[rows, lo:lo + half].astype(F32)
            x2 = ref[rows, lo + half:lo + 2 * half].astype(F32)
            return x1 * cos - x2 * sin, x2 * cos + x1 * sin

        for h in range(RET_HEADS):
            lo = h * RET_HEAD_DIM
            cols = slice(lo, lo + RET_HEAD_DIM)
            log_g = logg_ref[h]
            q1, q2 = rot(q_ref, lo)
            k1, k2 = rot(k_ref, lo)
            q = jnp.concatenate([q1, q2], axis=-1)
            k = jnp.concatenate([k1, k2], axis=-1) * scale
            v = v_ref[rows, cols].astype(BF16)

            k_decay = jnp.exp(log_g * ((c - 1.0) - t_vec))
            q_decay = jnp.exp(log_g * (t_vec + 1.0))
            chunk_decay = jnp.exp(log_g * float(c))

            scores = lax.dot_general(q.astype(BF16), k.astype(BF16), (((1,), (1,)), ((), ())),
                                     preferred_element_type=F32) * decay_ref[h]
            intra = jnp.dot(scores.astype(BF16), v, preferred_element_type=F32)
            state = state_ref[h]
            cross = jnp.dot((q * q_decay).astype(BF16), state.astype(BF16),
                            preferred_element_type=F32)
            kv = lax.dot_general((k * k_decay).astype(BF16), v, (((0,), (0,)), ((), ())),
                                 preferred_element_type=F32)
            state_ref[h] = chunk_decay * state + kv

            r = intra + cross
            mu = jnp.mean(r, axis=-1, keepdims=True)
            var = jnp.mean(jnp.square(r - mu), axis=-1, keepdims=True)
            y = ((r - mu) * lax.rsqrt(var + EPS)) * ng_ref[:, cols] + nb_ref[:, cols]
            gate = g_ref[rows, cols].astype(F32)
            o_ref[rows, cols] = (y * _silu(gate)).astype(o_ref.dtype)
        return carry

    lax.fori_loop(0, MIX_ROWS // c, chunk, 0)


def _retention(src, seq_len, log_g, cos, sin, ng, nb):
    arrs, specs = _slices(src, (6, 7, 8, 9), MIX_ROWS)
    rows = arrs[0].shape[0]
    half = RET_HEAD_DIM // 2
    kernel = functools.partial(_retention_kernel, seq_len // MIX_ROWS)
    tab_spec = pl.BlockSpec((MIX_ROWS, half), lambda i: (i, 0))
    return pl.pallas_call(
        kernel,
        grid=(rows // MIX_ROWS,),
        in_specs=[pl.BlockSpec(memory_space=pltpu.SMEM)] + specs + [
                  tab_spec, tab_spec, _row_spec(GROUP_W), _row_spec(GROUP_W)],
        out_specs=pl.BlockSpec((MIX_ROWS, GROUP_W), lambda i: (i, 0)),
        out_shape=jax.ShapeDtypeStruct((rows, GROUP_W), BF16),
        scratch_shapes=[pltpu.VMEM((RET_HEADS, RET_HEAD_DIM, RET_HEAD_DIM), F32),
                        pltpu.VMEM((RET_HEADS, RET_CHUNK, RET_CHUNK), F32)],
        compiler_params=_params(("arbitrary",), MIX_VMEM_LIMIT),
        name="retention",
    )(log_g, *arrs, cos, sin, ng.reshape(1, -1), nb.reshape(1, -1))


def _memattn_kernel(q_ref, g_ref, k_ref, v_ref, o_ref):
    scale = MEM_HEAD_DIM ** -0.5
    for h in range(MEM_HEADS):
        cols = slice(h * MEM_HEAD_DIM, (h + 1) * MEM_HEAD_DIM)
        s = lax.dot_general(q_ref[:, cols], k_ref[:, cols], (((1,), (1,)), ((), ())),
                            preferred_element_type=F32) * scale
        m = jnp.max(s, axis=-1, keepdims=True)
        p = jnp.exp(s - m)
        denom = jnp.sum(p, axis=-1, keepdims=True)
        p = p / denom
        o = jnp.dot(p.astype(BF16), v_ref[:, cols], preferred_element_type=F32)
        gate = g_ref[:, cols].astype(F32)
        o_ref[:, cols] = (o * _silu(gate)).astype(o_ref.dtype)


def _memattn(src, kv, seq_len):
    arrs, specs = _slices(src, (10, 11), MIX_ROWS)
    rows = arrs[0].shape[0]
    steps_per_seq = seq_len // MIX_ROWS
    return pl.pallas_call(
        _memattn_kernel,
        grid=(rows // MIX_ROWS,),
        in_specs=specs + [
                  pl.BlockSpec((MEM_LEN, GROUP_W), lambda i: (i // steps_per_seq, 0)),
                  pl.BlockSpec((MEM_LEN, GROUP_W), lambda i: (i // steps_per_seq, 1))],
        out_specs=pl.BlockSpec((MIX_ROWS, GROUP_W), lambda i: (i, 0)),
        out_shape=jax.ShapeDtypeStruct((rows, GROUP_W), BF16),
        compiler_params=_params(("parallel",), MIX_VMEM_LIMIT),
        name="memattn",
    )(*arrs, kv, kv)


def _rows2(start, n):
    return pl.ds(2 * start, n, stride=2)


def _derived_zero(x):
    bits = lax.bitcast_convert_type(x, jnp.uint32)
    bits = lax.shift_right_logical(lax.shift_right_logical(bits, jnp.uint32(16)), jnp.uint32(16))
    return lax.bitcast_convert_type(bits, F32)


def _conv_rows(seq_start, row0, unrolled, ca_ref, cb_ref, cg_ref, cw_ref, cbias_ref, cng_ref,
               cnb_ref, dst_ref, buf_ref):
    n_slabs = GROUP_W // LANES
    for s in range(n_slabs):
        cols = slice(s * LANES, (s + 1) * LANES)
        tail = buf_ref[s, _rows2(CONV_QR, CONV_HALO), :]
        buf_ref[s, _rows2(0, CONV_HALO), :] = jnp.where(seq_start, 0.0, tail)
        buf_ref[s, _rows2(CONV_HALO, CONV_QR), :] = (
            ca_ref[:, cols].astype(F32) * _sigmoid(cb_ref[:, cols].astype(F32)))

    prev_acc = [None]

    def chunk(c):
        r0 = c * CONV_RC
        for s in range(n_slabs):
            cols = slice(s * LANES, (s + 1) * LANES)
            acc = jnp.broadcast_to(cbias_ref[:, cols], (CONV_RC, LANES))
            if unrolled and prev_acc[0] is not None:
                acc = acc + _derived_zero(prev_acc[0])
            for k in range(CONV_WIDTH):
                start = r0 + (CONV_HALO - (CONV_WIDTH - 1) + k)
                acc = acc + cw_ref[k:k + 1, cols] * buf_ref[s, _rows2(start, CONV_RC), :]
            prev_acc[0] = acc
            mu = jnp.mean(acc, axis=-1, keepdims=True)
            var = jnp.mean(jnp.square(acc - mu), axis=-1, keepdims=True)
            y = ((acc - mu) * lax.rsqrt(var + EPS)) * cng_ref[:, cols] + cnb_ref[:, cols]
            if unrolled:
                gate = cg_ref[r0:r0 + CONV_RC, cols].astype(F32)
            else:
                gate = cg_ref[pl.ds(pl.multiple_of(r0, CONV_RC), CONV_RC), cols].astype(F32)
            out_rows = pl.ds(pl.multiple_of(row0 + r0, CONV_RC), CONV_RC)
            dst_ref[out_rows, cols] = (_silu(y) * _silu(gate)).astype(dst_ref.dtype)

    if unrolled:
        for c in range(CONV_QR // CONV_RC):
            chunk(c)
    else:
        def body(c, carry):
            chunk(c)
            return carry
        lax.fori_loop(0, CONV_QR // CONV_RC, body, 0)


def _outproj_kernel(emit_x, steps_per_seq, ya_ref, yc_ref, ym_ref, w_ref, x_ref, ng_ref,
                    ca_ref, cb_ref, cg_ref, cw_ref, cbias_ref, cng_ref, cnb_ref, *rest):
    if emit_x:
        xo_ref, no_ref, yb_ref, buf_ref, ssq_ref, xrow_ref = rest
    else:
        xo_ref = None
        no_ref, yb_ref, buf_ref, ssq_ref = rest
        xrow_ref = no_ref
    i = pl.program_id(0)
    j = pl.program_id(1)
    step = i * OUT_NJ + j
    seq_start = (step % steps_per_seq) == 0
    conv_args = (ca_ref, cb_ref, cg_ref, cw_ref, cbias_ref, cng_ref, cnb_ref)
    row0 = j * CONV_QR

    @pl.when(i == 0)
    def _():
        @pl.when(j == 0)
        def _():
            for s in range(GROUP_W // LANES):
                buf_ref[s, _rows2(CONV_QR, CONV_HALO), :] = jnp.zeros((CONV_HALO, LANES), F32)

        _conv_rows(seq_start, row0, False, *conv_args, yb_ref.at[0], buf_ref)

    @pl.when(i > 0)
    def _():
        acc = x_ref[...]
        ys = ((1, yb_ref[(i - 1) % 2]), (0, ya_ref[...]), (2, yc_ref[...]), (3, ym_ref[...]))
        for g, y in ys:
            acc = acc + jnp.dot(y, w_ref[g * GROUP_W:(g + 1) * GROUP_W, :],
                                preferred_element_type=F32)
        if emit_x:
            xo_ref[...] = acc
        col0 = pl.multiple_of(j * OUT_BN, OUT_BN)
        xrow_ref[:, pl.ds(col0, OUT_BN)] = acc
        ssq_ref[j] = jnp.sum(acc * acc, axis=-1, keepdims=True)

        _conv_rows(seq_start, row0, True, *conv_args, yb_ref.at[i % 2], buf_ref)

        @pl.when(j == OUT_NJ - 1)
        def _():
            ssq = ssq_ref[0]
            for jj in range(1, OUT_NJ):
                ssq = ssq + ssq_ref[jj]
            ssq_ref[0] = lax.rsqrt(ssq * (1.0 / D_MODEL) + EPS)

            for c in range(OUT_BM // NORM_RC):
                rows = slice(c * NORM_RC, (c + 1) * NORM_RC)
                rstd = ssq_ref[0, rows, :]
                for jj in range(OUT_NJ):
                    cols = slice(jj * OUT_BN, (jj + 1) * OUT_BN)
                    no_ref[rows, cols] = ((xrow_ref[rows, cols] * rstd)
                                          * ng_ref[:, cols]).astype(no_ref.dtype)


def _outproj(ys, src, seq_len, w_bf, x2d, next_g, conv_w, conv_b, conv_ng, conv_nb,
             emit_x, norm_dtype):
    conv_arrs, conv_cols = zip(*[src(s) for s in (3, 4, 5)])
    m, d = x2d.shape
    k = w_bf.shape[0]
    nb = m // OUT_BM
    steps_per_seq = seq_len // CONV_QR
    last_q = m // CONV_QR - 1

    def prev(i):
        return jnp.maximum(i - 1, 0)

    y_spec = pl.BlockSpec((OUT_BM, GROUP_W), lambda i, j: (prev(i), 0))

    def conv_spec(col):
        return pl.BlockSpec((CONV_QR, GROUP_W),
                            lambda i, j: (jnp.minimum(i * OUT_NJ + j, last_q), col))

    def tile_idx(i, j):
        return (prev(i), jnp.where(i == 0, 0, j))

    row_full = pl.BlockSpec((1, d), lambda i, j: (0, 0))
    row_grp = pl.BlockSpec((1, GROUP_W), lambda i, j: (0, 0))
    in_specs = [y_spec, y_spec, y_spec,
                pl.BlockSpec((k, OUT_BN), lambda i, j: (0, j)),
                pl.BlockSpec((OUT_BM, OUT_BN), tile_idx),
                row_full,
                conv_spec(conv_cols[0]), conv_spec(conv_cols[1]), conv_spec(conv_cols[2]),
                pl.BlockSpec((CONV_WIDTH, GROUP_W), lambda i, j: (0, 0)),
                row_grp, row_grp, row_grp]
    norm_spec = pl.BlockSpec((OUT_BM, d), lambda i, j: (prev(i), 0))
    norm_shape = jax.ShapeDtypeStruct((m, d), norm_dtype)
    if emit_x:
        out_specs = [pl.BlockSpec((OUT_BM, OUT_BN), tile_idx), norm_spec]
        out_shape = [jax.ShapeDtypeStruct((m, d), F32), norm_shape]
    else:
        out_specs = [norm_spec]
        out_shape = [norm_shape]
    scratch = [pltpu.VMEM((2, OUT_BM, GROUP_W), BF16),
               pltpu.VMEM((GROUP_W // LANES, 2 * (CONV_HALO + CONV_QR), LANES), F32),
               pltpu.VMEM((OUT_NJ, OUT_BM, 1), F32)]
    if emit_x:
        scratch.append(pltpu.VMEM((OUT_BM, d), F32))
    else:
        assert norm_dtype == F32
    ya, yc, ym = ys
    outs = pl.pallas_call(
        functools.partial(_outproj_kernel, emit_x, steps_per_seq),
        grid=(nb + 1, OUT_NJ),
        in_specs=in_specs,
        out_specs=out_specs,
        out_shape=out_shape,
        scratch_shapes=scratch,
        compiler_params=_params(("arbitrary", "arbitrary"), MM_VMEM_LIMIT),
        name="outproj",
    )(ya, yc, ym, w_bf, x2d, next_g.reshape(1, d), *conv_arrs,
      conv_w, conv_b.reshape(1, -1), conv_ng.reshape(1, -1), conv_nb.reshape(1, -1))
    return outs if emit_x else (None, outs[0])


def kernel(x, mem, positions, norm_g, w_in, sgu_norm_g, sgu_norm_b, sgu_w, sgu_b,
           conv_w, conv_b, conv_norm_g, conv_norm_b, ret_norm_g, ret_norm_b,
           mem_norm_g, w_mem_kv, w_out, final_norm_g):
    bsz, s_len, d = x.shape
    depth = w_in.shape[0]
    assert d == D_MODEL and w_in.shape[2] == N_IN_SLICES * GROUP_W
    assert s_len % OUT_BM == 0 and s_len % MIX_ROWS == 0
    assert MIX_ROWS % SGU_CHUNK == 0 and MIX_ROWS % RET_CHUNK == 0
    assert GROUP_W // CONV_GROUPS == LANES and CONV_HALO >= CONV_WIDTH - 1
    assert CONV_QR % CONV_RC == 0 and CONV_QR >= CONV_HALO
    rows = bsz * s_len
    x2d = x.reshape(rows, d)
    mem2d = mem.reshape(bsz * MEM_LEN, d)

    half = RET_HEAD_DIM // 2
    inv_freq = jnp.power(ROPE_BASE, -jnp.arange(half, dtype=F32) / half).reshape(1, half)
    pos_col = positions.astype(F32).reshape(rows, 1)
    cos, sin = _rope_tables(pos_col, inv_freq)
    log_g = jnp.log(1.0 - jnp.power(2.0, -5.0 - jnp.arange(RET_HEADS, dtype=F32)))

    h, first_slice = _norm_proj(x2d, norm_g[0], w_in[0, :, :IN_BN].astype(BF16))
    out = None
    for l in range(depth):
        last = l == depth - 1
        if l == 0:
            rest, w_out_bf = _inproj(h, w_in, l, BF16, side_stack=w_out, col_offset=1)

            def src(s, first_slice=first_slice, rest=rest):
                return (first_slice, 0) if s == 0 else (rest, s - 1)
        else:
            proj, w_out_bf = _inproj(h, w_in, l, BF16, side_stack=w_out)

            def src(s, proj=proj):
                return (proj, s)

        bias_full = jnp.repeat(sgu_b[l].T, GROUP_W // SGU_HEADS, axis=1)
        y_a = _sgu(src, sgu_norm_g[l], sgu_norm_b[l], sgu_w[l], bias_full)
        y_c = _retention(src, s_len, log_g, cos, sin, ret_norm_g[l], ret_norm_b[l])

        mem_n = _rmsnorm(mem2d, mem_norm_g[l], BF16)
        kv = _inproj(mem_n, w_mem_kv, l, BF16)
        y_m = _memattn(src, kv, s_len)

        next_g = final_norm_g if last else norm_g[l + 1]
        x2d, normed = _outproj((y_a, y_c, y_m), src, s_len, w_out_bf, x2d, next_g,
                               conv_w[l], conv_b[l], conv_norm_g[l], conv_norm_b[l],
                               emit_x=not last, norm_dtype=F32 if last else BF16)
        if last:
            out = normed
        else:
            h = normed
    return out.reshape(bsz, s_len, d)
```

```python
import functools

import jax
import jax.numpy as jnp
from jax import lax
from jax.experimental import pallas as pl
from jax.experimental.pallas import tpu as pltpu

F32 = jnp.float32
BF16 = jnp.bfloat16

D_MODEL = 4096
GROUP_W = 1024
N_IN_SLICES = 12
SGU_CHUNK = 128
SGU_HEADS = 8
CONV_WIDTH = 31
CONV_GROUPS = 8
RET_HEADS = 4
RET_HEAD_DIM = 256
MEM_LEN = 256
MEM_HEADS = 4
MEM_HEAD_DIM = 256
ROPE_BASE = 10000.0
EPS = 1e-6

V7X_VMEM_BYTES = 64 * 1024 * 1024
LANES = 128

NORM_ROWS = 512
FIRST_BM = 512
IN_BM = 1024
IN_BN = 1024
IN_MIN_ROW_TILES = 4
IN_STAGE_BYTES = 8 * 1024 * 1024
OUT_BM = 512
OUT_BN = 1024
OUT_NJ = D_MODEL // OUT_BN
CONV_QR = OUT_BM // OUT_NJ
CONV_RC = 16
CONV_HALO = 32
NORM_RC = 16
MIX_ROWS = 1024
RET_CHUNK = 256
MM_VMEM_LIMIT = 58 * 1024 * 1024
MIX_VMEM_LIMIT = 48 * 1024 * 1024


def _sigmoid(x):
    return 1.0 / (1.0 + jnp.exp(-x))


def _silu(x):
    return x * _sigmoid(x)


def _params(semantics, vmem_limit):
    return pltpu.CompilerParams(dimension_semantics=semantics, vmem_limit_bytes=vmem_limit)


def _rmsnorm_kernel(x_ref, g_ref, o_ref):
    x = x_ref[...]
    ms = jnp.mean(x * x, axis=-1, keepdims=True)
    o_ref[...] = ((x * lax.rsqrt(ms + EPS)) * g_ref[...]).astype(o_ref.dtype)


def _rmsnorm(x2d, g, out_dtype):
    rows, d = x2d.shape
    return pl.pallas_call(
        _rmsnorm_kernel,
        grid=(rows // NORM_ROWS,),
        in_specs=[pl.BlockSpec((NORM_ROWS, d), lambda i: (i, 0)),
                  pl.BlockSpec((1, d), lambda i: (0, 0))],
        out_specs=pl.BlockSpec((NORM_ROWS, d), lambda i: (i, 0)),
        out_shape=jax.ShapeDtypeStruct((rows, d), out_dtype),
        compiler_params=_params(("parallel",), MIX_VMEM_LIMIT),
        name="rmsnorm",
    )(x2d, g.reshape(1, d))


def _norm_proj_kernel(x_ref, g_ref, w_ref, h_ref, p_ref):
    for c in range(FIRST_BM // NORM_RC):
        rows = slice(c * NORM_RC, (c + 1) * NORM_RC)
        x = x_ref[rows, :]
        ms = jnp.mean(x * x, axis=-1, keepdims=True)
        h_ref[rows, :] = ((x * lax.rsqrt(ms + EPS)) * g_ref[...]).astype(h_ref.dtype)
    p_ref[...] = jnp.dot(h_ref[...], w_ref[...], preferred_element_type=F32).astype(p_ref.dtype)


def _norm_proj(x2d, g, w_bf):
    rows, d = x2d.shape
    n = w_bf.shape[1]
    bm = min(FIRST_BM, rows)
    assert bm == FIRST_BM
    return pl.pallas_call(
        _norm_proj_kernel,
        grid=(rows // bm,),
        in_specs=[pl.BlockSpec((bm, d), lambda i: (i, 0)),
                  pl.BlockSpec((1, d), lambda i: (0, 0)),
                  pl.BlockSpec((d, n), lambda i: (0, 0))],
        out_specs=[pl.BlockSpec((bm, d), lambda i: (i, 0)),
                   pl.BlockSpec((bm, n), lambda i: (i, 0))],
        out_shape=[jax.ShapeDtypeStruct((rows, d), BF16),
                   jax.ShapeDtypeStruct((rows, n), BF16)],
        compiler_params=_params(("parallel",), MM_VMEM_LIMIT),
        name="norm_proj",
    )(x2d, g.reshape(1, d), w_bf)


def _inproj_kernel(layer, col_offset, n_col_blocks, n_row_tiles, a_ref, w_hbm, o_ref, wb_ref,
                   stage_ref, sem):
    j = pl.program_id(0)
    i = pl.program_id(1)
    n_slots, kc, _ = stage_ref.shape

    def chunk_copy(col_block, c, slot):
        rows = pl.ds(pl.multiple_of(c * kc, kc), kc)
        cols = pl.ds(pl.multiple_of((col_block + col_offset) * IN_BN, IN_BN), IN_BN)
        return pltpu.make_async_copy(w_hbm.at[layer, rows, cols], stage_ref.at[slot], sem.at[slot])

    @pl.when((j == 0) & (i == 0))
    def _():
        copies = [chunk_copy(0, c, c % n_slots) for c in range(n_row_tiles)]
        for c in range(min(n_slots, n_row_tiles)):
            copies[c].start()
        for c in range(n_row_tiles):
            copies[c].wait()
            wb_ref[0, c * kc:(c + 1) * kc, :] = stage_ref[c % n_slots].astype(BF16)
            if c + n_slots < n_row_tiles:
                copies[c + n_slots].start()

        if n_col_blocks > 1:
            chunk_copy(1, 0, 0).start()

    o_ref[...] = jnp.dot(a_ref[...], wb_ref[j % 2], preferred_element_type=F32).astype(o_ref.dtype)

    @pl.when(j + 1 < n_col_blocks)
    def _():
        chunk_copy(j + 1, i, 0).wait()
        rows = pl.ds(pl.multiple_of(i * kc, kc), kc)
        wb_ref[(j + 1) % 2, rows, :] = stage_ref[0].astype(BF16)

        @pl.when(i + 1 < n_row_tiles)
        def _():
            chunk_copy(j + 1, i + 1, 0).start()

        @pl.when((i + 1 == n_row_tiles) & (j + 2 < n_col_blocks))
        def _():
            chunk_copy(j + 2, 0, 0).start()


def _inproj(a, w_stack, layer, out_dtype, col_offset=0):
    m, k = a.shape
    n = w_stack.shape[2] - col_offset * IN_BN
    bm = min(IN_BM, m // IN_MIN_ROW_TILES)
    n_row_tiles = m // bm
    n_col_blocks = n // IN_BN
    kc = k // n_row_tiles
    n_slots = max(2, min(4, IN_STAGE_BYTES // (kc * IN_BN * 4)))
    kernel = functools.partial(_inproj_kernel, layer, col_offset, n_col_blocks, n_row_tiles)
    return pl.pallas_call(
        kernel,
        grid=(n_col_blocks, n_row_tiles),
        in_specs=[pl.BlockSpec((bm, k), lambda j, i: (i, 0)), pl.BlockSpec(memory_space=pl.ANY)],
        out_specs=pl.BlockSpec((bm, IN_BN), lambda j, i: (i, j)),
        out_shape=jax.ShapeDtypeStruct((m, n), out_dtype),
        scratch_shapes=[pltpu.VMEM((2, k, IN_BN), BF16),
                        pltpu.VMEM((n_slots, kc, IN_BN), F32),
                        pltpu.SemaphoreType.DMA((n_slots,))],
        compiler_params=_params(("arbitrary", "arbitrary"), MM_VMEM_LIMIT),
        name="inproj",
    )(a, w_stack)


def _rope_kernel(pos_ref, freq_ref, cos_ref, sin_ref):
    ang = pos_ref[...] * freq_ref[...]
    cos_ref[...] = jnp.cos(ang)
    sin_ref[...] = jnp.sin(ang)


def _rope_tables(pos_col, inv_freq):
    rows = pos_col.shape[0]
    half = inv_freq.shape[1]
    spec = pl.BlockSpec((MIX_ROWS, half), lambda i: (i, 0))
    return pl.pallas_call(
        _rope_kernel,
        grid=(rows // MIX_ROWS,),
        in_specs=[pl.BlockSpec((MIX_ROWS, 1), lambda i: (i, 0)),
                  pl.BlockSpec((1, half), lambda i: (0, 0))],
        out_specs=[spec, spec],
        out_shape=[jax.ShapeDtypeStruct((rows, half), F32)] * 2,
        compiler_params=_params(("parallel",), MIX_VMEM_LIMIT),
        name="rope_tables",
    )(pos_col, inv_freq)


def _slice_spec(rows, col_block):
    return pl.BlockSpec((rows, GROUP_W), lambda i: (i, col_block))


def _slices(src, ids, rows):
    arrs, cols = zip(*[src(s) for s in ids])
    return list(arrs), [_slice_spec(rows, c) for c in cols]


def _row_spec(n):
    return pl.BlockSpec((1, n), lambda i: (0, 0))


def _sgu_kernel(u_ref, v_ref, g_ref, ng_ref, nb_ref, w_ref, bias_ref, o_ref, vn_ref):
    v = v_ref[...].astype(F32)
    mu = jnp.mean(v, axis=-1, keepdims=True)
    var = jnp.mean(jnp.square(v - mu), axis=-1, keepdims=True)
    vn = ((v - mu) * lax.rsqrt(var + EPS)) * ng_ref[...] + nb_ref[...]
    vn_ref[...] = vn.astype(BF16)

    t = lax.broadcasted_iota(jnp.int32, (SGU_CHUNK, SGU_CHUNK), 0)
    s = lax.broadcasted_iota(jnp.int32, (SGU_CHUNK, SGU_CHUNK), 1)
    causal = t >= s
    for h in range(SGU_HEADS):
        w = jnp.where(causal, w_ref[h], 0.0).astype(BF16)
        cols = slice(h * SGU_CHUNK, (h + 1) * SGU_CHUNK)
        for n in range(MIX_ROWS // SGU_CHUNK):
            rows = slice(n * SGU_CHUNK, (n + 1) * SGU_CHUNK)
            mix = jnp.dot(w, vn_ref[rows, cols], preferred_element_type=F32) + bias_ref[:, cols]
            u = u_ref[rows, cols].astype(F32)
            gate = g_ref[rows, cols].astype(F32)
            o_ref[rows, cols] = ((u * mix) * _silu(gate)).astype(o_ref.dtype)


def _sgu(src, ng, nb, w, bias_full):
    arrs, specs = _slices(src, (0, 1, 2), MIX_ROWS)
    rows = arrs[0].shape[0]
    return pl.pallas_call(
        _sgu_kernel,
        grid=(rows // MIX_ROWS,),
        in_specs=specs + [
                  _row_spec(GROUP_W), _row_spec(GROUP_W),
                  pl.BlockSpec((SGU_HEADS, SGU_CHUNK, SGU_CHUNK), lambda i: (0, 0, 0)),
                  pl.BlockSpec((SGU_CHUNK, GROUP_W), lambda i: (0, 0))],
        out_specs=pl.BlockSpec((MIX_ROWS, GROUP_W), lambda i: (i, 0)),
        out_shape=jax.ShapeDtypeStruct((rows, GROUP_W), BF16),
        scratch_shapes=[pltpu.VMEM((MIX_ROWS, GROUP_W), BF16)],
        compiler_params=_params(("parallel",), MIX_VMEM_LIMIT),
        name="sgu",
    )(*arrs, ng.reshape(1, -1), nb.reshape(1, -1), w, bias_full)


def _retention_kernel(steps_per_seq, logg_ref, q_ref, k_ref, v_ref, g_ref, cos_ref, sin_ref,
                      ng_ref, nb_ref, wside_ref, o_ref, wside_bf_ref, state_ref, decay_ref):
    i = pl.program_id(0)
    wside_bf_ref[...] = wside_ref[...].astype(BF16)

    @pl.when((i % steps_per_seq) == 0)
    def _():
        state_ref[...] = jnp.zeros_like(state_ref)

    c = RET_CHUNK
    half = RET_HEAD_DIM // 2
    scale = RET_HEAD_DIM ** -0.5
    t_row = lax.broadcasted_iota(jnp.int32, (c, c), 0)
    s_col = lax.broadcasted_iota(jnp.int32, (c, c), 1)
    diff = (t_row - s_col).astype(F32)
    t_vec = lax.broadcasted_iota(jnp.int32, (c, 1), 0).astype(F32)
    for h in range(RET_HEADS):
        log_g = logg_ref[h]
        decay_ref[h] = jnp.where(diff >= 0, jnp.exp(log_g * jnp.maximum(diff, 0.0)), 0.0)

    def chunk(ci, carry):
        rows = pl.ds(pl.multiple_of(ci * c, c), c)
        cos = cos_ref[rows, :]
        sin = sin_ref[rows, :]

        def rot(ref, lo):
            x1 = ref[rows, lo:lo + half].astype(F32)
            x2 = ref[rows, lo + half:lo + 2 * half].astype(F32)
            return x1 * cos - x2 * sin, x2 * cos + x1 * sin

        for h in range(RET_HEADS):
            lo = h * RET_HEAD_DIM
            cols = slice(lo, lo + RET_HEAD_DIM)
            log_g = logg_ref[h]
            q1, q2 = rot(q_ref, lo)
            k1, k2 = rot(k_ref, lo)
            q = jnp.concatenate([q1, q2], axis=-1)
            k = jnp.concatenate([k1, k2], axis=-1) * scale
            v = v_ref[rows, cols].astype(BF16)

            k_decay = jnp.exp(log_g * ((c - 1.0) - t_vec))
            q_decay = jnp.exp(log_g * (t_vec + 1.0))
            chunk_decay = jnp.exp(log_g * float(c))

            scores = lax.dot_general(q.astype(BF16), k.astype(BF16), (((1,), (1,)), ((), ())),
                                     preferred_element_type=F32) * decay_ref[h]
            intra = jnp.dot(scores.astype(BF16), v, preferred_element_type=F32)
            state = state_ref[h]
            cross = jnp.dot((q * q_decay).astype(BF16), state.astype(BF16),
                            preferred_element_type=F32)
            kv = lax.dot_general((k * k_decay).astype(BF16), v, (((0,), (0,)), ((), ())),
                                 preferred_element_type=F32)
            state_ref[h] = chunk_decay * state + kv

            r = intra + cross
            mu = jnp.mean(r, axis=-1, keepdims=True)
            var = jnp.mean(jnp.square(r - mu), axis=-1, keepdims=True)
            y = ((r - mu) * lax.rsqrt(var + EPS)) * ng_ref[:, cols] + nb_ref[:, cols]
            gate = g_ref[rows, cols].astype(F32)
            o_ref[rows, cols] = (y * _silu(gate)).astype(o_ref.dtype)
        return carry

    lax.fori_loop(0, MIX_ROWS // c, chunk, 0)


def _retention(src, seq_len, log_g, cos, sin, ng, nb, w_side_stack, layer):
    arrs, specs = _slices(src, (6, 7, 8, 9), MIX_ROWS)
    rows = arrs[0].shape[0]
    steps = rows // MIX_ROWS
    half = RET_HEAD_DIM // 2
    _, sr, sc = w_side_stack.shape
    assert sr % steps == 0
    kernel = functools.partial(_retention_kernel, seq_len // MIX_ROWS)
    tab_spec = pl.BlockSpec((MIX_ROWS, half), lambda i: (i, 0))
    return pl.pallas_call(
        kernel,
        grid=(steps,),
        in_specs=[pl.BlockSpec(memory_space=pltpu.SMEM)] + specs + [
                  tab_spec, tab_spec, _row_spec(GROUP_W), _row_spec(GROUP_W),
                  pl.BlockSpec((None, sr // steps, sc), lambda i: (layer, i, 0))],
        out_specs=[pl.BlockSpec((MIX_ROWS, GROUP_W), lambda i: (i, 0)),
                   pl.BlockSpec((sr // steps, sc), lambda i: (i, 0))],
        out_shape=[jax.ShapeDtypeStruct((rows, GROUP_W), BF16),
                   jax.ShapeDtypeStruct((sr, sc), BF16)],
        scratch_shapes=[pltpu.VMEM((RET_HEADS, RET_HEAD_DIM, RET_HEAD_DIM), F32),
                        pltpu.VMEM((RET_HEADS, RET_CHUNK, RET_CHUNK), F32)],
        compiler_params=_params(("arbitrary",), MIX_VMEM_LIMIT),
        name="retention",
    )(log_g, *arrs, cos, sin, ng.reshape(1, -1), nb.reshape(1, -1), w_side_stack)


def _memattn_kernel(q_ref, g_ref, k_ref, v_ref, o_ref):
    scale = MEM_HEAD_DIM ** -0.5
    for h in range(MEM_HEADS):
        cols = slice(h * MEM_HEAD_DIM, (h + 1) * MEM_HEAD_DIM)
        s = lax.dot_general(q_ref[:, cols], k_ref[:, cols], (((1,), (1,)), ((), ())),
                            preferred_element_type=F32) * scale
        m = jnp.max(s, axis=-1, keepdims=True)
        p = jnp.exp(s - m)
        denom = jnp.sum(p, axis=-1, keepdims=True)
        p = p / denom
        o = jnp.dot(p.astype(BF16), v_ref[:, cols], preferred_element_type=F32)
        gate = g_ref[:, cols].astype(F32)
        o_ref[:, cols] = (o * _silu(gate)).astype(o_ref.dtype)


def _memattn(src, kv, seq_len):
    arrs, specs = _slices(src, (10, 11), MIX_ROWS)
    rows = arrs[0].shape[0]
    steps_per_seq = seq_len // MIX_ROWS
    return pl.pallas_call(
        _memattn_kernel,
        grid=(rows // MIX_ROWS,),
        in_specs=specs + [
                  pl.BlockSpec((MEM_LEN, GROUP_W), lambda i: (i // steps_per_seq, 0)),
                  pl.BlockSpec((MEM_LEN, GROUP_W), lambda i: (i // steps_per_seq, 1))],
        out_specs=pl.BlockSpec((MIX_ROWS, GROUP_W), lambda i: (i, 0)),
        out_shape=jax.ShapeDtypeStruct((rows, GROUP_W), BF16),
        compiler_params=_params(("parallel",), MIX_VMEM_LIMIT),
        name="memattn",
    )(*arrs, kv, kv)


def _rows2(start, n):
    return pl.ds(2 * start, n, stride=2)


def _derived_zero(x):
    bits = lax.bitcast_convert_type(x, jnp.uint32)
    bits = lax.shift_right_logical(lax.shift_right_logical(bits, jnp.uint32(16)), jnp.uint32(16))
    return lax.bitcast_convert_type(bits, F32)


def _conv_rows(seq_start, row0, unrolled, ca_ref, cb_ref, cg_ref, cw_ref, cbias_ref, cng_ref,
               cnb_ref, dst_ref, buf_ref):
    n_slabs = GROUP_W // LANES
    for s in range(n_slabs):
        cols = slice(s * LANES, (s + 1) * LANES)
        tail = buf_ref[s, _rows2(CONV_QR, CONV_HALO), :]
        buf_ref[s, _rows2(0, CONV_HALO), :] = jnp.where(seq_start, 0.0, tail)
        buf_ref[s, _rows2(CONV_HALO, CONV_QR), :] = (
            ca_ref[:, cols].astype(F32) * _sigmoid(cb_ref[:, cols].astype(F32)))

    prev_acc = [None]

    def chunk(c):
        r0 = c * CONV_RC
        for s in range(n_slabs):
            cols = slice(s * LANES, (s + 1) * LANES)
            acc = jnp.broadcast_to(cbias_ref[:, cols], (CONV_RC, LANES))
            if unrolled and prev_acc[0] is not None:
                acc = acc + _derived_zero(prev_acc[0])
            for k in range(CONV_WIDTH):
                start = r0 + (CONV_HALO - (CONV_WIDTH - 1) + k)
                acc = acc + cw_ref[k:k + 1, cols] * buf_ref[s, _rows2(start, CONV_RC), :]
            prev_acc[0] = acc
            mu = jnp.mean(acc, axis=-1, keepdims=True)
            var = jnp.mean(jnp.square(acc - mu), axis=-1, keepdims=True)
            y = ((acc - mu) * lax.rsqrt(var + EPS)) * cng_ref[:, cols] + cnb_ref[:, cols]
            if unrolled:
                gate = cg_ref[r0:r0 + CONV_RC, cols].astype(F32)
            else:
                gate = cg_ref[pl.ds(pl.multiple_of(r0, CONV_RC), CONV_RC), cols].astype(F32)
            out_rows = pl.ds(pl.multiple_of(row0 + r0, CONV_RC), CONV_RC)
            dst_ref[out_rows, cols] = (_silu(y) * _silu(gate)).astype(dst_ref.dtype)

    if unrolled:
        for c in range(CONV_QR // CONV_RC):
            chunk(c)
    else:
        def body(c, carry):
            chunk(c)
            return carry
        lax.fori_loop(0, CONV_QR // CONV_RC, body, 0)


def _outproj_kernel(emit_x, steps_per_seq, ya_ref, yc_ref, ym_ref, w_ref, x_ref, ng_ref,
                    ca_ref, cb_ref, cg_ref, cw_ref, cbias_ref, cng_ref, cnb_ref, *rest):
    if emit_x:
        xo_ref, no_ref, yb_ref, buf_ref, ssq_ref, xrow_ref = rest
    else:
        xo_ref = None
        no_ref, yb_ref, buf_ref, ssq_ref = rest
        xrow_ref = no_ref
    i = pl.program_id(0)
    j = pl.program_id(1)
    step = i * OUT_NJ + j
    seq_start = (step % steps_per_seq) == 0
    conv_args = (ca_ref, cb_ref, cg_ref, cw_ref, cbias_ref, cng_ref, cnb_ref)
    row0 = j * CONV_QR

    @pl.when(i == 0)
    def _():
        @pl.when(j == 0)
        def _():
            for s in range(GROUP_W // LANES):
                buf_ref[s, _rows2(CONV_QR, CONV_HALO), :] = jnp.zeros((CONV_HALO, LANES), F32)

        _conv_rows(seq_start, row0, False, *conv_args, yb_ref.at[0], buf_ref)

    @pl.when(i > 0)
    def _():
        acc = x_ref[...]
        ys = ((1, yb_ref[(i - 1) % 2]), (0, ya_ref[...]), (2, yc_ref[...]), (3, ym_ref[...]))
        for g, y in ys:
            acc = acc + jnp.dot(y, w_ref[g * GROUP_W:(g + 1) * GROUP_W, :],
                                preferred_element_type=F32)
        if emit_x:
            xo_ref[...] = acc
        col0 = pl.multiple_of(j * OUT_BN, OUT_BN)
        xrow_ref[:, pl.ds(col0, OUT_BN)] = acc
        ssq_ref[j] = jnp.sum(acc * acc, axis=-1, keepdims=True)

        _conv_rows(seq_start, row0, True, *conv_args, yb_ref.at[i % 2], buf_ref)

        @pl.when(j == OUT_NJ - 1)
        def _():
            ssq = ssq_ref[0]
            for jj in range(1, OUT_NJ):
                ssq = ssq + ssq_ref[jj]
            ssq_ref[0] = lax.rsqrt(ssq * (1.0 / D_MODEL) + EPS)

            for c in range(OUT_BM // NORM_RC):
                rows = slice(c * NORM_RC, (c + 1) * NORM_RC)
                rstd = ssq_ref[0, rows, :]
                for jj in range(OUT_NJ):
                    cols = slice(jj * OUT_BN, (jj + 1) * OUT_BN)
                    no_ref[rows, cols] = ((xrow_ref[rows, cols] * rstd)
                                          * ng_ref[:, cols]).astype(no_ref.dtype)


def _outproj(ys, src, seq_len, w_bf, x2d, next_g, conv_w, conv_b, conv_ng, conv_nb,
             emit_x, norm_dtype):
    conv_arrs, conv_cols = zip(*[src(s) for s in (3, 4, 5)])
    m, d = x2d.shape
    k = w_bf.shape[0]
    nb = m // OUT_BM
    steps_per_seq = seq_len // CONV_QR
    last_q = m // CONV_QR - 1

    def prev(i):
        return jnp.maximum(i - 1, 0)

    y_spec = pl.BlockSpec((OUT_BM, GROUP_W), lambda i, j: (prev(i), 0))

    def conv_spec(col):
        return pl.BlockSpec((CONV_QR, GROUP_W),
                            lambda i, j: (jnp.minimum(i * OUT_NJ + j, last_q), col))

    def tile_idx(i, j):
        return (prev(i), jnp.where(i == 0, 0, j))

    row_full = pl.BlockSpec((1, d), lambda i, j: (0, 0))
    row_grp = pl.BlockSpec((1, GROUP_W), lambda i, j: (0, 0))
    in_specs = [y_spec, y_spec, y_spec,
                pl.BlockSpec((k, OUT_BN), lambda i, j: (0, j)),
                pl.BlockSpec((OUT_BM, OUT_BN), tile_idx),
                row_full,
                conv_spec(conv_cols[0]), conv_spec(conv_cols[1]), conv_spec(conv_cols[2]),
                pl.BlockSpec((CONV_WIDTH, GROUP_W), lambda i, j: (0, 0)),
                row_grp, row_grp, row_grp]
    norm_spec = pl.BlockSpec((OUT_BM, d), lambda i, j: (prev(i), 0))
    norm_shape = jax.ShapeDtypeStruct((m, d), norm_dtype)
    if emit_x:
        out_specs = [pl.BlockSpec((OUT_BM, OUT_BN), tile_idx), norm_spec]
        out_shape = [jax.ShapeDtypeStruct((m, d), F32), norm_shape]
    else:
        out_specs = [norm_spec]
        out_shape = [norm_shape]
    scratch = [pltpu.VMEM((2, OUT_BM, GROUP_W), BF16),
               pltpu.VMEM((GROUP_W // LANES, 2 * (CONV_HALO + CONV_QR), LANES), F32),
               pltpu.VMEM((OUT_NJ, OUT_BM, 1), F32)]
    if emit_x:
        scratch.append(pltpu.VMEM((OUT_BM, d), F32))
    else:
        assert norm_dtype == F32
    ya, yc, ym = ys
    outs = pl.pallas_call(
        functools.partial(_outproj_kernel, emit_x, steps_per_seq),
        grid=(nb + 1, OUT_NJ),
        in_specs=in_specs,
        out_specs=out_specs,
        out_shape=out_shape,
        scratch_shapes=scratch,
        compiler_params=_params(("arbitrary", "arbitrary"), MM_VMEM_LIMIT),
        name="outproj",
    )(ya, yc, ym, w_bf, x2d, next_g.reshape(1, d), *conv_arrs,
      conv_w, conv_b.reshape(1, -1), conv_ng.reshape(1, -1), conv_nb.reshape(1, -1))
    return outs if emit_x else (None, outs[0])


def kernel(x, mem, positions, norm_g, w_in, sgu_norm_g, sgu_norm_b, sgu_w, sgu_b,
           conv_w, conv_b, conv_norm_g, conv_norm_b, ret_norm_g, ret_norm_b,
           mem_norm_g, w_mem_kv, w_out, final_norm_g):
    bsz, s_len, d = x.shape
    depth = w_in.shape[0]
    assert d == D_MODEL and w_in.shape[2] == N_IN_SLICES * GROUP_W
    assert s_len % OUT_BM == 0 and s_len % MIX_ROWS == 0
    assert MIX_ROWS % SGU_CHUNK == 0 and MIX_ROWS % RET_CHUNK == 0
    assert GROUP_W // CONV_GROUPS == LANES and CONV_HALO >= CONV_WIDTH - 1
    assert CONV_QR % CONV_RC == 0 and CONV_QR >= CONV_HALO
    rows = bsz * s_len
    x2d = x.reshape(rows, d)
    mem2d = mem.reshape(bsz * MEM_LEN, d)

    half = RET_HEAD_DIM // 2
    inv_freq = jnp.power(ROPE_BASE, -jnp.arange(half, dtype=F32) / half).reshape(1, half)
    pos_col = positions.astype(F32).reshape(rows, 1)
    cos, sin = _rope_tables(pos_col, inv_freq)
    log_g = jnp.log(1.0 - jnp.power(2.0, -5.0 - jnp.arange(RET_HEADS, dtype=F32)))

    h, first_slice = _norm_proj(x2d, norm_g[0], w_in[0, :, :IN_BN].astype(BF16))
    out = None
    for l in range(depth):
        last = l == depth - 1
        if l == 0:
            rest = _inproj(h, w_in, l, BF16, col_offset=1)

            def src(s, first_slice=first_slice, rest=rest):
                return (first_slice, 0) if s == 0 else (rest, s - 1)
        else:
            proj = _inproj(h, w_in, l, BF16)

            def src(s, proj=proj):
                return (proj, s)

        bias_full = jnp.repeat(sgu_b[l].T, GROUP_W // SGU_HEADS, axis=1)
        y_a = _sgu(src, sgu_norm_g[l], sgu_norm_b[l], sgu_w[l], bias_full)
        y_c, w_out_bf = _retention(src, s_len, log_g, cos, sin, ret_norm_g[l], ret_norm_b[l],
                                   w_out, l)

        mem_n = _rmsnorm(mem2d, mem_norm_g[l], BF16)
        kv = _inproj(mem_n, w_mem_kv, l, BF16)
        y_m = _memattn(src, kv, s_len)

        next_g = final_norm_g if last else norm_g[l + 1]
        x2d, normed = _outproj((y_a, y_c, y_m), src, s_len, w_out_bf, x2d, next_g,
                               conv_w[l], conv_b[l], conv_norm_g[l], conv_norm_b[l],
                               emit_x=not last, norm_dtype=F32 if last else BF16)
        if last:
            out = normed
        else:
            h = normed
    return out.reshape(bsz, s_len, d)
```

```python
import functools

import jax
import jax.numpy as jnp
from jax import lax
from jax.experimental import pallas as pl
from jax.experimental.pallas import tpu as pltpu

F32 = jnp.float32
BF16 = jnp.bfloat16

D_MODEL = 4096
GROUP_W = 1024
N_IN_SLICES = 12
SGU_CHUNK = 128
SGU_HEADS = 8
CONV_WIDTH = 31
CONV_GROUPS = 8
RET_HEADS = 4
RET_HEAD_DIM = 256
MEM_LEN = 256
MEM_HEADS = 4
MEM_HEAD_DIM = 256
ROPE_BASE = 10000.0
EPS = 1e-6

V7X_VMEM_BYTES = 64 * 1024 * 1024
LANES = 128

NORM_ROWS = 512
FIRST_BM = 512
IN_BM = 1024
IN_BN = 1024
IN_MIN_ROW_TILES = 4
IN_STAGE_BYTES = 8 * 1024 * 1024
OUT_BM = 512
OUT_BN = 1024
OUT_NJ = D_MODEL // OUT_BN
CONV_QR = OUT_BM // OUT_NJ
CONV_RC = 16
CONV_HALO = 32
NORM_RC = 16
MIX_ROWS = 1024
RET_CHUNK = 256
MM_VMEM_LIMIT = 58 * 1024 * 1024
MIX_VMEM_LIMIT = 48 * 1024 * 1024


def _sigmoid(x):
    return 1.0 / (1.0 + jnp.exp(-x))


def _silu(x):
    return x * _sigmoid(x)


def _params(semantics, vmem_limit):
    return pltpu.CompilerParams(dimension_semantics=semantics, vmem_limit_bytes=vmem_limit)


def _rmsnorm_kernel(x_ref, g_ref, o_ref):
    x = x_ref[...]
    ms = jnp.mean(x * x, axis=-1, keepdims=True)
    o_ref[...] = ((x * lax.rsqrt(ms + EPS)) * g_ref[...]).astype(o_ref.dtype)


def _rmsnorm(x2d, g, out_dtype):
    rows, d = x2d.shape
    return pl.pallas_call(
        _rmsnorm_kernel,
        grid=(rows // NORM_ROWS,),
        in_specs=[pl.BlockSpec((NORM_ROWS, d), lambda i: (i, 0)),
                  pl.BlockSpec((1, d), lambda i: (0, 0))],
        out_specs=pl.BlockSpec((NORM_ROWS, d), lambda i: (i, 0)),
        out_shape=jax.ShapeDtypeStruct((rows, d), out_dtype),
        compiler_params=_params(("parallel",), MIX_VMEM_LIMIT),
        name="rmsnorm",
    )(x2d, g.reshape(1, d))


def _norm_proj_kernel(x_ref, g_ref, w_ref, h_ref, p_ref):
    for c in range(FIRST_BM // NORM_RC):
        rows = slice(c * NORM_RC, (c + 1) * NORM_RC)
        x = x_ref[rows, :]
        ms = jnp.mean(x * x, axis=-1, keepdims=True)
        h_ref[rows, :] = ((x * lax.rsqrt(ms + EPS)) * g_ref[...]).astype(h_ref.dtype)
    p_ref[...] = jnp.dot(h_ref[...], w_ref[...], preferred_element_type=F32).astype(p_ref.dtype)


def _norm_proj(x2d, g, w_bf):
    rows, d = x2d.shape
    n = w_bf.shape[1]
    bm = min(FIRST_BM, rows)
    assert bm == FIRST_BM
    return pl.pallas_call(
        _norm_proj_kernel,
        grid=(rows // bm,),
        in_specs=[pl.BlockSpec((bm, d), lambda i: (i, 0)),
                  pl.BlockSpec((1, d), lambda i: (0, 0)),
                  pl.BlockSpec((d, n), lambda i: (0, 0))],
        out_specs=[pl.BlockSpec((bm, d), lambda i: (i, 0)),
                   pl.BlockSpec((bm, n), lambda i: (i, 0))],
        out_shape=[jax.ShapeDtypeStruct((rows, d), BF16),
                   jax.ShapeDtypeStruct((rows, n), BF16)],
        compiler_params=_params(("parallel",), MM_VMEM_LIMIT),
        name="norm_proj",
    )(x2d, g.reshape(1, d), w_bf)


def _inproj_kernel(layer, col_offset, n_col_blocks, n_row_tiles, a_ref, w_hbm, o_ref, wb_ref,
                   stage_ref, sem):
    j = pl.program_id(0)
    i = pl.program_id(1)
    n_slots, kc, _ = stage_ref.shape

    def chunk_copy(col_block, c, slot):
        rows = pl.ds(pl.multiple_of(c * kc, kc), kc)
        cols = pl.ds(pl.multiple_of((col_block + col_offset) * IN_BN, IN_BN), IN_BN)
        return pltpu.make_async_copy(w_hbm.at[layer, rows, cols], stage_ref.at[slot], sem.at[slot])

    @pl.when((j == 0) & (i == 0))
    def _():
        copies = [chunk_copy(0, c, c % n_slots) for c in range(n_row_tiles)]
        for c in range(min(n_slots, n_row_tiles)):
            copies[c].start()
        for c in range(n_row_tiles):
            copies[c].wait()
            wb_ref[0, c * kc:(c + 1) * kc, :] = stage_ref[c % n_slots].astype(BF16)
            if c + n_slots < n_row_tiles:
                copies[c + n_slots].start()

        if n_col_blocks > 1:
            chunk_copy(1, 0, 0).start()

    o_ref[...] = jnp.dot(a_ref[...], wb_ref[j % 2], preferred_element_type=F32).astype(o_ref.dtype)

    @pl.when(j + 1 < n_col_blocks)
    def _():
        chunk_copy(j + 1, i, 0).wait()
        rows = pl.ds(pl.multiple_of(i * kc, kc), kc)
        wb_ref[(j + 1) % 2, rows, :] = stage_ref[0].astype(BF16)

        @pl.when(i + 1 < n_row_tiles)
        def _():
            chunk_copy(j + 1, i + 1, 0).start()

        @pl.when((i + 1 == n_row_tiles) & (j + 2 < n_col_blocks))
        def _():
            chunk_copy(j + 2, 0, 0).start()


def _inproj(a, w_stack, layer, out_dtype, col_offset=0):
    m, k = a.shape
    n = w_stack.shape[2] - col_offset * IN_BN
    bm = min(IN_BM, m // IN_MIN_ROW_TILES)
    n_row_tiles = m // bm
    n_col_blocks = n // IN_BN
    kc = k // n_row_tiles
    n_slots = max(2, min(4, IN_STAGE_BYTES // (kc * IN_BN * 4)))
    kernel = functools.partial(_inproj_kernel, layer, col_offset, n_col_blocks, n_row_tiles)
    return pl.pallas_call(
        kernel,
        grid=(n_col_blocks, n_row_tiles),
        in_specs=[pl.BlockSpec((bm, k), lambda j, i: (i, 0)), pl.BlockSpec(memory_space=pl.ANY)],
        out_specs=pl.BlockSpec((bm, IN_BN), lambda j, i: (i, j)),
        out_shape=jax.ShapeDtypeStruct((m, n), out_dtype),
        scratch_shapes=[pltpu.VMEM((2, k, IN_BN), BF16),
                        pltpu.VMEM((n_slots, kc, IN_BN), F32),
                        pltpu.SemaphoreType.DMA((n_slots,))],
        compiler_params=_params(("arbitrary", "arbitrary"), MM_VMEM_LIMIT),
        name="inproj",
    )(a, w_stack)


def _rope_kernel(pos_ref, freq_ref, cos_ref, sin_ref):
    ang = pos_ref[...] * freq_ref[...]
    cos_ref[...] = jnp.cos(ang)
    sin_ref[...] = jnp.sin(ang)


def _rope_tables(pos_col, inv_freq):
    rows = pos_col.shape[0]
    half = inv_freq.shape[1]
    spec = pl.BlockSpec((MIX_ROWS, half), lambda i: (i, 0))
    return pl.pallas_call(
        _rope_kernel,
        grid=(rows // MIX_ROWS,),
        in_specs=[pl.BlockSpec((MIX_ROWS, 1), lambda i: (i, 0)),
                  pl.BlockSpec((1, half), lambda i: (0, 0))],
        out_specs=[spec, spec],
        out_shape=[jax.ShapeDtypeStruct((rows, half), F32)] * 2,
        compiler_params=_params(("parallel",), MIX_VMEM_LIMIT),
        name="rope_tables",
    )(pos_col, inv_freq)


def _slice_spec(rows, col_block):
    return pl.BlockSpec((rows, GROUP_W), lambda i: (i, col_block))


def _slices(src, ids, rows):
    arrs, cols = zip(*[src(s) for s in ids])
    return list(arrs), [_slice_spec(rows, c) for c in cols]


def _row_spec(n):
    return pl.BlockSpec((1, n), lambda i: (0, 0))


def _sgu_kernel(u_ref, v_ref, g_ref, ng_ref, nb_ref, w_ref, bias_ref, o_ref, vn_ref):
    v = v_ref[...].astype(F32)
    mu = jnp.mean(v, axis=-1, keepdims=True)
    var = jnp.mean(jnp.square(v - mu), axis=-1, keepdims=True)
    vn = ((v - mu) * lax.rsqrt(var + EPS)) * ng_ref[...] + nb_ref[...]
    vn_ref[...] = vn.astype(BF16)

    t = lax.broadcasted_iota(jnp.int32, (SGU_CHUNK, SGU_CHUNK), 0)
    s = lax.broadcasted_iota(jnp.int32, (SGU_CHUNK, SGU_CHUNK), 1)
    causal = t >= s
    for h in range(SGU_HEADS):
        w = jnp.where(causal, w_ref[h], 0.0).astype(BF16)
        cols = slice(h * SGU_CHUNK, (h + 1) * SGU_CHUNK)
        for n in range(MIX_ROWS // SGU_CHUNK):
            rows = slice(n * SGU_CHUNK, (n + 1) * SGU_CHUNK)
            mix = jnp.dot(w, vn_ref[rows, cols], preferred_element_type=F32) + bias_ref[:, cols]
            u = u_ref[rows, cols].astype(F32)
            gate = g_ref[rows, cols].astype(F32)
            o_ref[rows, cols] = ((u * mix) * _silu(gate)).astype(o_ref.dtype)


def _retention_kernel(steps_per_seq, logg_ref, q_ref, k_ref, v_ref, g_ref, cos_ref, sin_ref,
                      ng_ref, nb_ref, wside_ref, o_ref, wside_bf_ref, state_ref, decay_ref):
    i = pl.program_id(0)
    wside_bf_ref[...] = wside_ref[...].astype(BF16)

    @pl.when((i % steps_per_seq) == 0)
    def _():
        state_ref[...] = jnp.zeros_like(state_ref)

    c = RET_CHUNK
    half = RET_HEAD_DIM // 2
    scale = RET_HEAD_DIM ** -0.5
    t_row = lax.broadcasted_iota(jnp.int32, (c, c), 0)
    s_col = lax.broadcasted_iota(jnp.int32, (c, c), 1)
    diff = (t_row - s_col).astype(F32)
    t_vec = lax.broadcasted_iota(jnp.int32, (c, 1), 0).astype(F32)
    for h in range(RET_HEADS):
        log_g = logg_ref[h]
        decay_ref[h] = jnp.where(diff >= 0, jnp.exp(log_g * jnp.maximum(diff, 0.0)), 0.0)

    def chunk(ci, carry):
        rows = pl.ds(pl.multiple_of(ci * c, c), c)
        cos = cos_ref[rows, :]
        sin = sin_ref[rows, :]

        def rot(ref, lo):
            x1 = ref[rows, lo:lo + half].astype(F32)
            x2 = ref[rows, lo + half:lo + 2 * half].astype(F32)
            return x1 * cos - x2 * sin, x2 * cos + x1 * sin

        for h in range(RET_HEADS):
            lo = h * RET_HEAD_DIM
            cols = slice(lo, lo + RET_HEAD_DIM)
            log_g = logg_ref[h]
            q1, q2 = rot(q_ref, lo)
            k1, k2 = rot(k_ref, lo)
            q = jnp.concatenate([q1, q2], axis=-1)
            k = jnp.concatenate([k1, k2], axis=-1) * scale
            v = v_ref[rows, cols].astype(BF16)

            k_decay = jnp.exp(log_g * ((c - 1.0) - t_vec))
            q_decay = jnp.exp(log_g * (t_vec + 1.0))
            chunk_decay = jnp.exp(log_g * float(c))

            scores = lax.dot_general(q.astype(BF16), k.astype(BF16), (((1,), (1,)), ((), ())),
                                     preferred_element_type=F32) * decay_ref[h]
            intra = jnp.dot(scores.astype(BF16), v, preferred_element_type=F32)
            state = state_ref[h]
            cross = jnp.dot((q * q_decay).astype(BF16), state.astype(BF16),
                            preferred_element_type=F32)
            kv = lax.dot_general((k * k_decay).astype(BF16), v, (((0,), (0,)), ((), ())),
                                 preferred_element_type=F32)
            state_ref[h] = chunk_decay * state + kv

            r = intra + cross
            mu = jnp.mean(r, axis=-1, keepdims=True)
            var = jnp.mean(jnp.square(r - mu), axis=-1, keepdims=True)
            y = ((r - mu) * lax.rsqrt(var + EPS)) * ng_ref[:, cols] + nb_ref[:, cols]
            gate = g_ref[rows, cols].astype(F32)
            o_ref[rows, cols] = (y * _silu(gate)).astype(o_ref.dtype)
        return carry

    lax.fori_loop(0, MIX_ROWS // c, chunk, 0)


def _retention(src, seq_len, log_g, cos, sin, ng, nb, w_side_stack, layer):
    arrs, specs = _slices(src, (6, 7, 8, 9), MIX_ROWS)
    rows = arrs[0].shape[0]
    steps = rows // MIX_ROWS
    half = RET_HEAD_DIM // 2
    _, sr, sc = w_side_stack.shape
    assert sr % steps == 0
    kernel = functools.partial(_retention_kernel, seq_len // MIX_ROWS)
    tab_spec = pl.BlockSpec((MIX_ROWS, half), lambda i: (i, 0))
    return pl.pallas_call(
        kernel,
        grid=(steps,),
        in_specs=[pl.BlockSpec(memory_space=pltpu.SMEM)] + specs + [
                  tab_spec, tab_spec, _row_spec(GROUP_W), _row_spec(GROUP_W),
                  pl.BlockSpec((None, sr // steps, sc), lambda i: (layer, i, 0))],
        out_specs=[pl.BlockSpec((MIX_ROWS, GROUP_W), lambda i: (i, 0)),
                   pl.BlockSpec((sr // steps, sc), lambda i: (i, 0))],
        out_shape=[jax.ShapeDtypeStruct((rows, GROUP_W), BF16),
                   jax.ShapeDtypeStruct((sr, sc), BF16)],
        scratch_shapes=[pltpu.VMEM((RET_HEADS, RET_HEAD_DIM, RET_HEAD_DIM), F32),
                        pltpu.VMEM((RET_HEADS, RET_CHUNK, RET_CHUNK), F32)],
        compiler_params=_params(("arbitrary",), MIX_VMEM_LIMIT),
        name="retention",
    )(log_g, *arrs, cos, sin, ng.reshape(1, -1), nb.reshape(1, -1), w_side_stack)


def _memattn_kernel(q_ref, g_ref, k_ref, v_ref, o_ref):
    scale = MEM_HEAD_DIM ** -0.5
    for h in range(MEM_HEADS):
        cols = slice(h * MEM_HEAD_DIM, (h + 1) * MEM_HEAD_DIM)
        s = lax.dot_general(q_ref[:, cols], k_ref[:, cols], (((1,), (1,)), ((), ())),
                            preferred_element_type=F32) * scale
        m = jnp.max(s, axis=-1, keepdims=True)
        p = jnp.exp(s - m)
        denom = jnp.sum(p, axis=-1, keepdims=True)
        p = p / denom
        o = jnp.dot(p.astype(BF16), v_ref[:, cols], preferred_element_type=F32)
        gate = g_ref[:, cols].astype(F32)
        o_ref[:, cols] = (o * _silu(gate)).astype(o_ref.dtype)


def _sgu_memattn_kernel(u_ref, v_ref, ga_ref, ng_ref, nb_ref, w_ref, bias_ref,
                        q_ref, gm_ref, k_ref, vm_ref, oa_ref, om_ref, vn_ref):
    _sgu_kernel(u_ref, v_ref, ga_ref, ng_ref, nb_ref, w_ref, bias_ref, oa_ref, vn_ref)
    _memattn_kernel(q_ref, gm_ref, k_ref, vm_ref, om_ref)


def _sgu_memattn(src, ng, nb, w, bias_full, kv, seq_len):
    arrs_a, specs_a = _slices(src, (0, 1, 2), MIX_ROWS)
    arrs_m, specs_m = _slices(src, (10, 11), MIX_ROWS)
    rows = arrs_a[0].shape[0]
    steps_per_seq = seq_len // MIX_ROWS
    out_spec = pl.BlockSpec((MIX_ROWS, GROUP_W), lambda i: (i, 0))
    out_shape = jax.ShapeDtypeStruct((rows, GROUP_W), BF16)
    return pl.pallas_call(
        _sgu_memattn_kernel,
        grid=(rows // MIX_ROWS,),
        in_specs=specs_a + [
                  _row_spec(GROUP_W), _row_spec(GROUP_W),
                  pl.BlockSpec((SGU_HEADS, SGU_CHUNK, SGU_CHUNK), lambda i: (0, 0, 0)),
                  pl.BlockSpec((SGU_CHUNK, GROUP_W), lambda i: (0, 0))] + specs_m + [
                  pl.BlockSpec((MEM_LEN, GROUP_W), lambda i: (i // steps_per_seq, 0)),
                  pl.BlockSpec((MEM_LEN, GROUP_W), lambda i: (i // steps_per_seq, 1))],
        out_specs=[out_spec, out_spec],
        out_shape=[out_shape, out_shape],
        scratch_shapes=[pltpu.VMEM((MIX_ROWS, GROUP_W), BF16)],
        compiler_params=_params(("parallel",), MIX_VMEM_LIMIT),
        name="sgu_memattn",
    )(*arrs_a, ng.reshape(1, -1), nb.reshape(1, -1), w, bias_full, *arrs_m, kv, kv)


def _rows2(start, n):
    return pl.ds(2 * start, n, stride=2)


def _derived_zero(x):
    bits = lax.bitcast_convert_type(x, jnp.uint32)
    bits = lax.shift_right_logical(lax.shift_right_logical(bits, jnp.uint32(16)), jnp.uint32(16))
    return lax.bitcast_convert_type(bits, F32)


def _conv_rows(seq_start, row0, unrolled, ca_ref, cb_ref, cg_ref, cw_ref, cbias_ref, cng_ref,
               cnb_ref, dst_ref, buf_ref):
    n_slabs = GROUP_W // LANES
    for s in range(n_slabs):
        cols = slice(s * LANES, (s + 1) * LANES)
        tail = buf_ref[s, _rows2(CONV_QR, CONV_HALO), :]
        buf_ref[s, _rows2(0, CONV_HALO), :] = jnp.where(seq_start, 0.0, tail)
        buf_ref[s, _rows2(CONV_HALO, CONV_QR), :] = (
            ca_ref[:, cols].astype(F32) * _sigmoid(cb_ref[:, cols].astype(F32)))

    prev_acc = [None]

    def chunk(c):
        r0 = c * CONV_RC
        for s in range(n_slabs):
            cols = slice(s * LANES, (s + 1) * LANES)
            acc = jnp.broadcast_to(cbias_ref[:, cols], (CONV_RC, LANES))
            if unrolled and prev_acc[0] is not None:
                acc = acc + _derived_zero(prev_acc[0])
            for k in range(CONV_WIDTH):
                start = r0 + (CONV_HALO - (CONV_WIDTH - 1) + k)
                acc = acc + cw_ref[k:k + 1, cols] * buf_ref[s, _rows2(start, CONV_RC), :]
            prev_acc[0] = acc
            mu = jnp.mean(acc, axis=-1, keepdims=True)
            var = jnp.mean(jnp.square(acc - mu), axis=-1, keepdims=True)
            y = ((acc - mu) * lax.rsqrt(var + EPS)) * cng_ref[:, cols] + cnb_ref[:, cols]
            if unrolled:
                gate = cg_ref[r0:r0 + CONV_RC, cols].astype(F32)
            else:
                gate = cg_ref[pl.ds(pl.multiple_of(r0, CONV_RC), CONV_RC), cols].astype(F32)
            out_rows = pl.ds(pl.multiple_of(row0 + r0, CONV_RC), CONV_RC)
            dst_ref[out_rows, cols] = (_silu(y) * _silu(gate)).astype(dst_ref.dtype)

    if unrolled:
        for c in range(CONV_QR // CONV_RC):
            chunk(c)
    else:
        def body(c, carry):
            chunk(c)
            return carry
        lax.fori_loop(0, CONV_QR // CONV_RC, body, 0)


def _outproj_kernel(emit_x, steps_per_seq, ya_ref, yc_ref, ym_ref, w_ref, x_ref, ng_ref,
                    ca_ref, cb_ref, cg_ref, cw_ref, cbias_ref, cng_ref, cnb_ref, *rest):
    if emit_x:
        xo_ref, no_ref, yb_ref, buf_ref, ssq_ref, xrow_ref = rest
    else:
        xo_ref = None
        no_ref, yb_ref, buf_ref, ssq_ref = rest
        xrow_ref = no_ref
    i = pl.program_id(0)
    j = pl.program_id(1)
    step = i * OUT_NJ + j
    seq_start = (step % steps_per_seq) == 0
    conv_args = (ca_ref, cb_ref, cg_ref, cw_ref, cbias_ref, cng_ref, cnb_ref)
    row0 = j * CONV_QR

    @pl.when(i == 0)
    def _():
        @pl.when(j == 0)
        def _():
            for s in range(GROUP_W // LANES):
                buf_ref[s, _rows2(CONV_QR, CONV_HALO), :] = jnp.zeros((CONV_HALO, LANES), F32)

        _conv_rows(seq_start, row0, False, *conv_args, yb_ref.at[0], buf_ref)

    @pl.when(i > 0)
    def _():
        acc = x_ref[...]
        ys = ((1, yb_ref[(i - 1) % 2]), (0, ya_ref[...]), (2, yc_ref[...]), (3, ym_ref[...]))
        for g, y in ys:
            acc = acc + jnp.dot(y, w_ref[g * GROUP_W:(g + 1) * GROUP_W, :],
                                preferred_element_type=F32)
        if emit_x:
            xo_ref[...] = acc
        col0 = pl.multiple_of(j * OUT_BN, OUT_BN)
        xrow_ref[:, pl.ds(col0, OUT_BN)] = acc
        ssq_ref[j] = jnp.sum(acc * acc, axis=-1, keepdims=True)

        _conv_rows(seq_start, row0, True, *conv_args, yb_ref.at[i % 2], buf_ref)

        @pl.when(j == OUT_NJ - 1)
        def _():
            ssq = ssq_ref[0]
            for jj in range(1, OUT_NJ):
                ssq = ssq + ssq_ref[jj]
            ssq_ref[0] = lax.rsqrt(ssq * (1.0 / D_MODEL) + EPS)

            for c in range(OUT_BM // NORM_RC):
                rows = slice(c * NORM_RC, (c + 1) * NORM_RC)
                rstd = ssq_ref[0, rows, :]
                for jj in range(OUT_NJ):
                    cols = slice(jj * OUT_BN, (jj + 1) * OUT_BN)
                    no_ref[rows, cols] = ((xrow_ref[rows, cols] * rstd)
                                          * ng_ref[:, cols]).astype(no_ref.dtype)


def _outproj(ys, src, seq_len, w_bf, x2d, next_g, conv_w, conv_b, conv_ng, conv_nb,
             emit_x, norm_dtype):
    conv_arrs, conv_cols = zip(*[src(s) for s in (3, 4, 5)])
    m, d = x2d.shape
    k = w_bf.shape[0]
    nb = m // OUT_BM
    steps_per_seq = seq_len // CONV_QR
    last_q = m // CONV_QR - 1

    def prev(i):
        return jnp.maximum(i - 1, 0)

    y_spec = pl.BlockSpec((OUT_BM, GROUP_W), lambda i, j: (prev(i), 0))

    def conv_spec(col):
        return pl.BlockSpec((CONV_QR, GROUP_W),
                            lambda i, j: (jnp.minimum(i * OUT_NJ + j, last_q), col))

    def tile_idx(i, j):
        return (prev(i), jnp.where(i == 0, 0, j))

    row_full = pl.BlockSpec((1, d), lambda i, j: (0, 0))
    row_grp = pl.BlockSpec((1, GROUP_W), lambda i, j: (0, 0))
    in_specs = [y_spec, y_spec, y_spec,
                pl.BlockSpec((k, OUT_BN), lambda i, j: (0, j)),
                pl.BlockSpec((OUT_BM, OUT_BN), tile_idx),
                row_full,
                conv_spec(conv_cols[0]), conv_spec(conv_cols[1]), conv_spec(conv_cols[2]),
                pl.BlockSpec((CONV_WIDTH, GROUP_W), lambda i, j: (0, 0)),
                row_grp, row_grp, row_grp]
    norm_spec = pl.BlockSpec((OUT_BM, d), lambda i, j: (prev(i), 0))
    norm_shape = jax.ShapeDtypeStruct((m, d), norm_dtype)
    if emit_x:
        out_specs = [pl.BlockSpec((OUT_BM, OUT_BN), tile_idx), norm_spec]
        out_shape = [jax.ShapeDtypeStruct((m, d), F32), norm_shape]
    else:
        out_specs = [norm_spec]
        out_shape = [norm_shape]
    scratch = [pltpu.VMEM((2, OUT_BM, GROUP_W), BF16),
               pltpu.VMEM((GROUP_W // LANES, 2 * (CONV_HALO + CONV_QR), LANES), F32),
               pltpu.VMEM((OUT_NJ, OUT_BM, 1), F32)]
    if emit_x:
        scratch.append(pltpu.VMEM((OUT_BM, d), F32))
    else:
        assert norm_dtype == F32
    ya, yc, ym = ys
    outs = pl.pallas_call(
        functools.partial(_outproj_kernel, emit_x, steps_per_seq),
        grid=(nb + 1, OUT_NJ),
        in_specs=in_specs,
        out_specs=out_specs,
        out_shape=out_shape,
        scratch_shapes=scratch,
        compiler_params=_params(("arbitrary", "arbitrary"), MM_VMEM_LIMIT),
        name="outproj",
    )(ya, yc, ym, w_bf, x2d, next_g.reshape(1, d), *conv_arrs,
      conv_w, conv_b.reshape(1, -1), conv_ng.reshape(1, -1), conv_nb.reshape(1, -1))
    return outs if emit_x else (None, outs[0])


def kernel(x, mem, positions, norm_g, w_in, sgu_norm_g, sgu_norm_b, sgu_w, sgu_b,
           conv_w, conv_b, conv_norm_g, conv_norm_b, ret_norm_g, ret_norm_b,
           mem_norm_g, w_mem_kv, w_out, final_norm_g):
    bsz, s_len, d = x.shape
    depth = w_in.shape[0]
    assert d == D_MODEL and w_in.shape[2] == N_IN_SLICES * GROUP_W
    assert s_len % OUT_BM == 0 and s_len % MIX_ROWS == 0
    assert MIX_ROWS % SGU_CHUNK == 0 and MIX_ROWS % RET_CHUNK == 0
    assert GROUP_W // CONV_GROUPS == LANES and CONV_HALO >= CONV_WIDTH - 1
    assert CONV_QR % CONV_RC == 0 and CONV_QR >= CONV_HALO
    rows = bsz * s_len
    x2d = x.reshape(rows, d)
    mem2d = mem.reshape(bsz * MEM_LEN, d)

    half = RET_HEAD_DIM // 2
    inv_freq = jnp.power(ROPE_BASE, -jnp.arange(half, dtype=F32) / half).reshape(1, half)
    pos_col = positions.astype(F32).reshape(rows, 1)
    cos, sin = _rope_tables(pos_col, inv_freq)
    log_g = jnp.log(1.0 - jnp.power(2.0, -5.0 - jnp.arange(RET_HEADS, dtype=F32)))

    h, first_slice = _norm_proj(x2d, norm_g[0], w_in[0, :, :IN_BN].astype(BF16))
    out = None
    for l in range(depth):
        last = l == depth - 1
        if l == 0:
            rest = _inproj(h, w_in, l, BF16, col_offset=1)

            def src(s, first_slice=first_slice, rest=rest):
                return (first_slice, 0) if s == 0 else (rest, s - 1)
        else:
            proj = _inproj(h, w_in, l, BF16)

            def src(s, proj=proj):
                return (proj, s)

        bias_full = jnp.repeat(sgu_b[l].T, GROUP_W // SGU_HEADS, axis=1)
        y_c, w_out_bf = _retention(src, s_len, log_g, cos, sin, ret_norm_g[l], ret_norm_b[l],
                                   w_out, l)

        mem_n = _rmsnorm(mem2d, mem_norm_g[l], BF16)
        kv = _inproj(mem_n, w_mem_kv, l, BF16)
        y_a, y_m = _sgu_memattn(src, sgu_norm_g[l], sgu_norm_b[l], sgu_w[l], bias_full, kv, s_len)

        next_g = final_norm_g if last else norm_g[l + 1]
        x2d, normed = _outproj((y_a, y_c, y_m), src, s_len, w_out_bf, x2d, next_g,
                               conv_w[l], conv_b[l], conv_norm_g[l], conv_norm_b[l],
                               emit_x=not last, norm_dtype=F32 if last else BF16)
        if last:
            out = normed
        else:
            h = normed
    return out.reshape(bsz, s_len, d)
```
